```python
import jax, jax.numpy as jnp
from jax import lax
import numpy as np

D_MODEL = 2048
BATCH = 4
SEQ = 2048
DEPTH = 4
DEC_BATCH = 32
DEC_SEQ = 8
PAST_LEN = 16384
PAGE_SIZE = 128

N_MIXERS = 2
HEAD_DIM = 64
N_HEADS = D_MODEL // HEAD_DIM
N_KV_HEADS = 8
GROUP = N_HEADS // N_KV_HEADS
Q_DIM = N_HEADS * HEAD_DIM
KV_DIM = N_KV_HEADS * HEAD_DIM
QKV_DIM = Q_DIM + 2 * KV_DIM
WINDOW = 128
BLOCK = WINDOW
ROPE_THETA = 10000.0
CONV_WIDTH = 31
D_CONV = D_MODEL
D_FF = 4 * D_MODEL
RMS_EPS = 1e-6
LN_EPS = 1e-5

kernel_name = "hybrid_swa_sink_conformer_conv_step"


def rms_norm(x, g):
    x32 = x.astype(jnp.float32)
    y = x32 * lax.rsqrt(jnp.mean(jnp.square(x32), axis=-1, keepdims=True) + RMS_EPS)
    return (y * g.astype(jnp.float32)).astype(x.dtype)


def layer_norm(x, g, b):
    x32 = x.astype(jnp.float32)
    mu = jnp.mean(x32, axis=-1, keepdims=True)
    xc = x32 - mu
    var = jnp.mean(jnp.square(xc), axis=-1, keepdims=True)
    y = xc * lax.rsqrt(var + LN_EPS) * g.astype(jnp.float32) + b.astype(jnp.float32)
    return y.astype(x.dtype)


def rope_tables(pos):
    inv = ROPE_THETA ** (-jnp.arange(0, HEAD_DIM, 2, dtype=jnp.float32) / HEAD_DIM)
    ang = pos.astype(jnp.float32)[:, None] * inv[None, :]
    return jnp.cos(ang), jnp.sin(ang)


def apply_rope(x, cos, sin):
    shape = (1, cos.shape[0]) + (1,) * (x.ndim - 3) + (HEAD_DIM // 2,)
    c = cos.reshape(shape)
    s = sin.reshape(shape)
    x32 = x.astype(jnp.float32)
    x1, x2 = x32[..., :HEAD_DIM // 2], x32[..., HEAD_DIM // 2:]
    return jnp.concatenate([x1 * c - x2 * s, x2 * c + x1 * s], axis=-1).astype(x.dtype)


def qkv_project(h, w_qkv, b_qkv, pos):
    b, s, _ = h.shape
    qkv = h @ w_qkv + b_qkv
    q = qkv[..., :Q_DIM].reshape(b, s, N_KV_HEADS, GROUP, HEAD_DIM)
    k = qkv[..., Q_DIM:Q_DIM + KV_DIM].reshape(b, s, N_KV_HEADS, HEAD_DIM)
    v = qkv[..., Q_DIM + KV_DIM:].reshape(b, s, N_KV_HEADS, HEAD_DIM)
    cos, sin = rope_tables(pos)
    return apply_rope(q, cos, sin), apply_rope(k, cos, sin), v


def sink_softmax(s, mask, sinks):
    sink = sinks.astype(jnp.float32).reshape(N_KV_HEADS, GROUP, 1, 1)
    s = jnp.where(mask, s, -jnp.inf)
    m = jnp.maximum(jnp.max(s, axis=-1, keepdims=True), sink)
    p = jnp.exp(s - m)
    denom = jnp.sum(p, axis=-1, keepdims=True) + jnp.exp(sink - m)
    return p / denom


def swa_attention_prompt(h, w_qkv, b_qkv, sinks, w_o, b_o):
    b, s, _ = h.shape
    nb = s // BLOCK
    pos = jnp.arange(s, dtype=jnp.int32)
    q, k, v = qkv_project(h, w_qkv, b_qkv, pos)
    qb = q.reshape(b, nb, BLOCK, N_KV_HEADS, GROUP, HEAD_DIM)
    kb = k.reshape(b, nb, BLOCK, N_KV_HEADS, HEAD_DIM)
    vb = v.reshape(b, nb, BLOCK, N_KV_HEADS, HEAD_DIM)
    pad = ((0, 0), (1, 0), (0, 0), (0, 0), (0, 0))
    k_band = jnp.concatenate([jnp.pad(kb, pad)[:, :-1], kb], axis=2)
    v_band = jnp.concatenate([jnp.pad(vb, pad)[:, :-1], vb], axis=2)
    scores = jnp.einsum('bnqhgd,bnchd->bnhgqc', qb, k_band,
                        preferred_element_type=jnp.float32) * (HEAD_DIM ** -0.5)
    qpos = pos.reshape(nb, BLOCK)
    kpos = jnp.concatenate([qpos - BLOCK, qpos], axis=1)
    diff = qpos[:, :, None] - kpos[:, None, :]
    mask = (diff >= 0) & (diff <= WINDOW) & (kpos[:, None, :] >= 0)
    p = sink_softmax(scores, mask[None, :, None, None], sinks)
    o = jnp.einsum('bnhgqc,bnchd->bnqhgd', p.astype(v.dtype), v_band)
    out = o.reshape(b, s, Q_DIM) @ w_o + b_o
    return out, k[:, -WINDOW:], v[:, -WINDOW:]


def swa_attention_sample(h, cache_k, cache_v, w_qkv, b_qkv, sinks, w_o, b_o):
    b, t, _ = h.shape
    w = cache_k.shape[1]
    qpos = PAST_LEN + jnp.arange(t, dtype=jnp.int32)
    q, k, v = qkv_project(h, w_qkv, b_qkv, qpos)
    k_all = jnp.concatenate([cache_k.astype(k.dtype), k], axis=1)
    v_all = jnp.concatenate([cache_v.astype(v.dtype), v], axis=1)
    scores = jnp.einsum('bqhgd,bchd->bhgqc', q, k_all,
                        preferred_element_type=jnp.float32) * (HEAD_DIM ** -0.5)
    kpos = PAST_LEN - w + jnp.arange(w + t, dtype=jnp.int32)
    diff = qpos[:, None] - kpos[None, :]
    mask = (diff >= 0) & (diff <= WINDOW)
    p = sink_softmax(scores, mask, sinks)
    o = jnp.einsum('bhgqc,bchd->bqhgd', p.astype(v.dtype), v_all)
    out = o.reshape(b, t, Q_DIM) @ w_o + b_o
    return out, k_all[:, -w:], v_all[:, -w:]


def conformer_conv(h, left, w_in, b_in, w_dw, b_dw, ln_g, ln_b, w_out, b_out):
    a, gate = jnp.split(h @ w_in + b_in, 2, axis=-1)
    u = a * jax.nn.sigmoid(gate)
    u_pad = jnp.concatenate([left.astype(u.dtype), u], axis=1)
    c = lax.conv_general_dilated(
        u_pad, w_dw.astype(u.dtype)[:, None, :], window_strides=(1,), padding='VALID',
        dimension_numbers=('NWC', 'WIO', 'NWC'), feature_group_count=D_CONV) + b_dw
    c = layer_norm(c, ln_g, ln_b)
    c = c * jax.nn.sigmoid(c)
    return c @ w_out + b_out, u_pad[:, -(CONV_WIDTH - 1):]


def sq_relu_mlp(h, w_up, w_down):
    return jnp.square(jax.nn.relu(h @ w_up)) @ w_down


def setup_inputs(seed: int = 0) -> dict:
    key = jax.random.key(seed)
    ks = jax.random.split(key, 24)
    n_attn = (DEPTH + N_MIXERS - 1) // N_MIXERS
    n_conv = DEPTH // N_MIXERS
    win_rows = min(WINDOW, PAST_LEN)

    def nrm(k, shape, scale):
        return scale * jax.random.normal(k, shape, jnp.float32)

    return {
        "x_prompt": nrm(ks[0], (BATCH, SEQ, D_MODEL), 1.0),
        "x_sample": nrm(ks[1], (DEC_BATCH, DEC_SEQ, D_MODEL), 1.0),
        "cache_k": nrm(ks[2], (n_attn, DEC_BATCH, win_rows, N_KV_HEADS, HEAD_DIM), 1.0),
        "cache_v": nrm(ks[3], (n_attn, DEC_BATCH, win_rows, N_KV_HEADS, HEAD_DIM), 1.0),
        "state_conv": nrm(ks[4], (n_conv, DEC_BATCH, CONV_WIDTH - 1, D_CONV), 0.5),
        "norm_mix": 1.0 + nrm(ks[5], (DEPTH, D_MODEL), 0.02),
        "norm_mlp": 1.0 + nrm(ks[6], (DEPTH, D_MODEL), 0.02),
        "norm_final": 1.0 + nrm(ks[7], (D_MODEL,), 0.02),
        "attn_w_qkv": nrm(ks[8], (n_attn, D_MODEL, QKV_DIM), D_MODEL ** -0.5),
        "attn_b_qkv": nrm(ks[9], (n_attn, QKV_DIM), 0.02),
        "attn_sinks": nrm(ks[10], (n_attn, N_HEADS), 1.0),
        "attn_w_o": nrm(ks[11], (n_attn, Q_DIM, D_MODEL), Q_DIM ** -0.5),
        "attn_b_o": nrm(ks[12], (n_attn, D_MODEL), 0.02),
        "conv_w_in": nrm(ks[13], (n_conv, D_MODEL, 2 * D_CONV), D_MODEL ** -0.5),
        "conv_b_in": nrm(ks[14], (n_conv, 2 * D_CONV), 0.02),
        "conv_w_dw": nrm(ks[15], (n_conv, CONV_WIDTH, D_CONV), CONV_WIDTH ** -0.5),
        "conv_b_dw": nrm(ks[16], (n_conv, D_CONV), 0.02),
        "conv_ln_g": 1.0 + nrm(ks[17], (n_conv, D_CONV), 0.02),
        "conv_ln_b": nrm(ks[18], (n_conv, D_CONV), 0.02),
        "conv_w_out": nrm(ks[19], (n_conv, D_CONV, D_MODEL), D_CONV ** -0.5),
        "conv_b_out": nrm(ks[20], (n_conv, D_MODEL), 0.02),
        "mlp_w_up": nrm(ks[21], (DEPTH, D_MODEL, D_FF), D_MODEL ** -0.5),
        "mlp_w_down": nrm(ks[22], (DEPTH, D_FF, D_MODEL), D_FF ** -0.5),
    }


def reference(x_prompt, x_sample, cache_k, cache_v, state_conv,
              norm_mix, norm_mlp, norm_final,
              attn_w_qkv, attn_b_qkv, attn_sinks, attn_w_o, attn_b_o,
              conv_w_in, conv_b_in, conv_w_dw, conv_b_dw, conv_ln_g, conv_ln_b,
              conv_w_out, conv_b_out, mlp_w_up, mlp_w_down):
    xp, xs = x_prompt, x_sample
    k_p, v_p, c_p, k_s, v_s, c_s = [], [], [], [], [], []
    for i in range(DEPTH):
        j = i // N_MIXERS
        hp = rms_norm(xp, norm_mix[i])
        hs = rms_norm(xs, norm_mix[i])
        if i % N_MIXERS == 0:
            mp, kp_new, vp_new = swa_attention_prompt(
                hp, attn_w_qkv[j], attn_b_qkv[j], attn_sinks[j], attn_w_o[j], attn_b_o[j])
            ms, ks_new, vs_new = swa_attention_sample(
                hs, cache_k[j], cache_v[j], attn_w_qkv[j], attn_b_qkv[j], attn_sinks[j],
                attn_w_o[j], attn_b_o[j])
            k_p.append(kp_new)
            v_p.append(vp_new)
            k_s.append(ks_new)
            v_s.append(vs_new)
        else:
            zeros_left = jnp.zeros((hp.shape[0], CONV_WIDTH - 1, D_CONV), hp.dtype)
            mp, cp_new = conformer_conv(
                hp, zeros_left, conv_w_in[j], conv_b_in[j], conv_w_dw[j], conv_b_dw[j],
                conv_ln_g[j], conv_ln_b[j], conv_w_out[j], conv_b_out[j])
            ms, cs_new = conformer_conv(
                hs, state_conv[j], conv_w_in[j], conv_b_in[j], conv_w_dw[j], conv_b_dw[j],
                conv_ln_g[j], conv_ln_b[j], conv_w_out[j], conv_b_out[j])
            c_p.append(cp_new)
            c_s.append(cs_new)
        xp = xp + mp
        xs = xs + ms
        xp = xp + sq_relu_mlp(rms_norm(xp, norm_mlp[i]), mlp_w_up[i], mlp_w_down[i])
        xs = xs + sq_relu_mlp(rms_norm(xs, norm_mlp[i]), mlp_w_up[i], mlp_w_down[i])
    y_prompt = rms_norm(xp, norm_final)
    y_sample = rms_norm(xs, norm_final)
    return (y_prompt, y_sample,
            jnp.stack(k_p), jnp.stack(v_p), jnp.stack(c_p),
            jnp.stack(k_s), jnp.stack(v_s), jnp.stack(c_s))
```

```python
import functools

import jax
import jax.numpy as jnp
from jax import lax
from jax.experimental import pallas as pl
from jax.experimental.pallas import tpu as pltpu

D_MODEL = 2048
HEAD_DIM = 64
N_HEADS = 32
N_KV_HEADS = 8
GROUP = N_HEADS // N_KV_HEADS
Q_DIM = N_HEADS * HEAD_DIM
KV_DIM = N_KV_HEADS * HEAD_DIM
WINDOW = 128
PAST_LEN = 16384
ROPE_THETA = 10000.0
CONV_WIDTH = 31
D_FF = 4 * D_MODEL
RMS_EPS = 1e-6
LN_EPS = 1e-5
DEPTH = 4

VMEM_LIMIT_BYTES = 60 * 1024 * 1024
LANES = 128
ROW_TILE = 1024
COL_TILE = 512
FF_TILE = 512
CONV_ROWS = 128
CONV_HALO = 32

BF16 = jnp.bfloat16
F32 = jnp.float32


def _params(*sem):
    return pltpu.CompilerParams(dimension_semantics=sem, vmem_limit_bytes=VMEM_LIMIT_BYTES)


def _rms_rows(x, g):
    ms = jnp.mean(x * x, axis=-1, keepdims=True)
    return x * lax.rsqrt(ms + RMS_EPS) * g


def _qkv_kernel(x_ref, g_ref, w_ref, b_ref, cos_ref, sin_ref, q_ref, kv_ref, h_ref):
    j = pl.program_id(1)
    nq = Q_DIM // COL_TILE

    @pl.when(j == 0)
    def _():
        h_ref[...] = _rms_rows(x_ref[...], g_ref[...]).astype(BF16)

    y = jnp.dot(h_ref[...], w_ref[...].astype(BF16), preferred_element_type=F32) + b_ref[...]

    @pl.when(j <= nq)
    def _():
        lane = lax.broadcasted_iota(jnp.int32, y.shape, 1)
        first_half = (lane % HEAD_DIM) < (HEAD_DIM // 2)
        rot = jnp.where(first_half,
                        pltpu.roll(y, COL_TILE - HEAD_DIM // 2, 1),
                        pltpu.roll(y, HEAD_DIM // 2, 1))
        r = y * cos_ref[...] + rot * sin_ref[...]

        @pl.when(j < nq)
        def _():
            q_ref[...] = (r * (HEAD_DIM ** -0.5)).astype(BF16)

        @pl.when(j == nq)
        def _():
            kv_ref[...] = r

    @pl.when(j == nq + 1)
    def _():
        kv_ref[...] = y


def _qkv_rope(x, g, w, b, cos_t, sin_t):
    m = x.shape[0]
    tm = min(ROW_TILE, m)
    n_tab = cos_t.shape[0] // tm
    nq = Q_DIM // COL_TILE
    grid = (m // tm, (Q_DIM + 2 * KV_DIM) // COL_TILE)
    return pl.pallas_call(
        _qkv_kernel,
        grid=grid,
        in_specs=[
            pl.BlockSpec((tm, D_MODEL), lambda i, j: (i, 0)),
            pl.BlockSpec((1, D_MODEL), lambda i, j: (0, 0)),
            pl.BlockSpec((D_MODEL, COL_TILE), lambda i, j: (0, j)),
            pl.BlockSpec((1, COL_TILE), lambda i, j: (0, j)),
            pl.BlockSpec((tm, COL_TILE), lambda i, j: (i % n_tab, 0)),
            pl.BlockSpec((tm, COL_TILE), lambda i, j: (i % n_tab, 0)),
        ],
        out_specs=[
            pl.BlockSpec((tm, COL_TILE), lambda i, j: (i, jnp.minimum(j, nq - 1))),
            pl.BlockSpec((tm, COL_TILE), lambda i, j: (i, jnp.maximum(j - nq, 0))),
        ],
        out_shape=[
            jax.ShapeDtypeStruct((m, Q_DIM), BF16),
            jax.ShapeDtypeStruct((m, 2 * KV_DIM), F32),
        ],
        scratch_shapes=[pltpu.VMEM((tm, D_MODEL), BF16)],
        compiler_params=_params("parallel", "arbitrary"),
        name="qkv_rope",
    )(x, g.reshape(1, D_MODEL), w, b.reshape(1, -1), cos_t, sin_t)


def _attn_prompt_kernel(sinks_ref, q_ref, kc_ref, kp_ref, vc_ref, vp_ref, o_ref):
    n = pl.program_id(1)
    qi = lax.broadcasted_iota(jnp.int32, (WINDOW, WINDOW), 0)
    kj = lax.broadcasted_iota(jnp.int32, (WINDOW, WINDOW), 1)
    mask_cur = kj <= qi
    mask_prev = jnp.logical_and(kj >= qi, n > 0)
    nt = (((1,), (1,)), ((), ()))
    for h in range(N_KV_HEADS):
        ks = slice(h * HEAD_DIM, (h + 1) * HEAD_DIM)
        kc = kc_ref[:, ks].astype(BF16)
        kp = kp_ref[:, ks].astype(BF16)
        vc = vc_ref[:, ks].astype(BF16)
        vp = vp_ref[:, ks].astype(BF16)
        for g in range(GROUP):
            hh = h * GROUP + g
            qs = slice(hh * HEAD_DIM, (hh + 1) * HEAD_DIM)
            qh = q_ref[:, qs]
            sc = lax.dot_general(qh, kc, nt, preferred_element_type=F32)
            sp = lax.dot_general(qh, kp, nt, preferred_element_type=F32)
            sc = jnp.where(mask_cur, sc, -jnp.inf)
            sp = jnp.where(mask_prev, sp, -jnp.inf)
            sink = sinks_ref[hh]
            mx = jnp.maximum(jnp.max(sc, axis=-1, keepdims=True),
                             jnp.max(sp, axis=-1, keepdims=True))
            mx = jnp.maximum(mx, sink)
            pc = jnp.exp(sc - mx)
            pp = jnp.exp(sp - mx)
            denom = (jnp.sum(pc, axis=-1, keepdims=True) + jnp.sum(pp, axis=-1, keepdims=True)
                     + jnp.exp(sink - mx))
            o = (jnp.dot(pc.astype(BF16), vc, preferred_element_type=F32)
                 + jnp.dot(pp.astype(BF16), vp, preferred_element_type=F32))
            o_ref[:, qs] = (o / denom).astype(BF16)


def _attn_prompt(q, kv, sinks, batch, seq):
    nb = seq // WINDOW
    kcol = KV_DIM // KV_DIM

    def cur(b, n):
        return b * nb + n

    def prev(b, n):
        return b * nb + jnp.maximum(n - 1, 0)

    return pl.pallas_call(
        _attn_prompt_kernel,
        grid=(batch, nb),
        in_specs=[
            pl.BlockSpec(memory_space=pltpu.SMEM),
            pl.BlockSpec((WINDOW, Q_DIM), lambda b, n: (cur(b, n), 0)),
            pl.BlockSpec((WINDOW, KV_DIM), lambda b, n: (cur(b, n), 0)),
            pl.BlockSpec((WINDOW, KV_DIM), lambda b, n: (prev(b, n), 0)),
            pl.BlockSpec((WINDOW, KV_DIM), lambda b, n: (cur(b, n), kcol)),
            pl.BlockSpec((WINDOW, KV_DIM), lambda b, n: (prev(b, n), kcol)),
        ],
        out_specs=pl.BlockSpec((WINDOW, Q_DIM), lambda b, n: (cur(b, n), 0)),
        out_shape=jax.ShapeDtypeStruct((batch * seq, Q_DIM), BF16),
        compiler_params=_params("parallel", "arbitrary"),
        name="attn_prompt",
    )(sinks, q, kv, kv, kv, kv)


SAMPLE_BATCH_BLOCK = 2
SAMPLE_KEY_PAD = 8


def _attn_sample_kernel(sinks_ref, q_ref, kvn_ref, ck_ref, cv_ref, o_ref, *, t_new):
    n_keys = WINDOW + t_new + SAMPLE_KEY_PAD
    rows = GROUP * t_new
    t_of_row = lax.broadcasted_iota(jnp.int32, (rows, n_keys), 0) % t_new
    col = lax.broadcasted_iota(jnp.int32, (rows, n_keys), 1)
    mask = jnp.logical_or(
        jnp.logical_and(col < WINDOW, col >= t_of_row),
        jnp.logical_and(col >= WINDOW, col - WINDOW <= t_of_row))
    nt = (((1,), (1,)), ((), ()))
    q_all = q_ref[...].astype(F32)
    kvn = kvn_ref[...]
    zpad = jnp.zeros((SAMPLE_KEY_PAD, HEAD_DIM), F32)
    for bb in range(SAMPLE_BATCH_BLOCK):
        r0 = bb * t_new
        for h in range(N_KV_HEADS):
            ks = slice(h * HEAD_DIM, (h + 1) * HEAD_DIM)
            vs = slice(KV_DIM + h * HEAD_DIM, KV_DIM + (h + 1) * HEAD_DIM)
            k_all = jnp.concatenate([ck_ref[bb, :, ks], kvn[r0:r0 + t_new, ks], zpad], axis=0).astype(BF16)
            v_all = jnp.concatenate([cv_ref[bb, :, ks], kvn[r0:r0 + t_new, vs], zpad], axis=0).astype(BF16)
            qh = jnp.concatenate(
                [q_all[r0:r0 + t_new, (h * GROUP + g) * HEAD_DIM:(h * GROUP + g + 1) * HEAD_DIM]
                 for g in range(GROUP)], axis=0).astype(BF16)
            sink = jnp.concatenate(
                [jnp.full((t_new, 1), sinks_ref[h * GROUP + g], F32) for g in range(GROUP)], axis=0)
            s = lax.dot_general(qh, k_all, nt, preferred_element_type=F32)
            s = jnp.where(mask, s, -jnp.inf)
            mx = jnp.maximum(jnp.max(s, axis=-1, keepdims=True), sink)
            p = jnp.exp(s - mx)
            denom = jnp.sum(p, axis=-1, keepdims=True) + jnp.exp(sink - mx)
            o = jnp.dot(p.astype(BF16), v_all, preferred_element_type=F32) / denom
            for g in range(GROUP):
                hh = h * GROUP + g
                o_ref[r0:r0 + t_new, hh * HEAD_DIM:(hh + 1) * HEAD_DIM] = o[g * t_new:(g + 1) * t_new]


def _attn_sample(q, kv, cache_k, cache_v, sinks, batch, t_new):
    rows = SAMPLE_BATCH_BLOCK * t_new
    return pl.pallas_call(
        functools.partial(_attn_sample_kernel, t_new=t_new),
        grid=(batch // SAMPLE_BATCH_BLOCK,),
        in_specs=[
            pl.BlockSpec(memory_space=pltpu.SMEM),
            pl.BlockSpec((rows, Q_DIM), lambda b: (b, 0)),
            pl.BlockSpec((rows, 2 * KV_DIM), lambda b: (b, 0)),
            pl.BlockSpec((SAMPLE_BATCH_BLOCK, WINDOW, KV_DIM), lambda b: (b, 0, 0)),
            pl.BlockSpec((SAMPLE_BATCH_BLOCK, WINDOW, KV_DIM), lambda b: (b, 0, 0)),
        ],
        out_specs=pl.BlockSpec((rows, Q_DIM), lambda b: (b, 0)),
        out_shape=jax.ShapeDtypeStruct((batch * t_new, Q_DIM), F32),
        compiler_params=_params("parallel"),
        name="attn_sample",
    )(sinks, q, kv, cache_k, cache_v)


def _proj_res_kernel(a_ref, w_ref, b_ref, res_ref, o_ref):
    y = jnp.dot(a_ref[...].astype(BF16), w_ref[...].astype(BF16), preferred_element_type=F32)
    o_ref[...] = res_ref[...] + y + b_ref[...]


def _proj_res(a, w, b, res):
    m, k = a.shape
    n = w.shape[1]
    tm = min(ROW_TILE, m)
    return pl.pallas_call(
        _proj_res_kernel,
        grid=(m // tm, n // COL_TILE),
        in_specs=[
            pl.BlockSpec((tm, k), lambda i, j: (i, 0)),
            pl.BlockSpec((k, COL_TILE), lambda i, j: (0, j)),
            pl.BlockSpec((1, COL_TILE), lambda i, j: (0, j)),
            pl.BlockSpec((tm, COL_TILE), lambda i, j: (i, j)),
        ],
        out_specs=pl.BlockSpec((tm, COL_TILE), lambda i, j: (i, j)),
        out_shape=jax.ShapeDtypeStruct((m, n), F32),
        compiler_params=_params("parallel", "arbitrary"),
        name="proj_res",
    )(a, w, b.reshape(1, n), res)


def _mlp_kernel(x_ref, g_ref, wu_ref, wd_ref, o_ref, h_ref):
    f = pl.program_id(1)

    @pl.when(f == 0)
    def _():
        x = x_ref[...]
        h_ref[...] = _rms_rows(x, g_ref[...]).astype(BF16)
        o_ref[...] = x

    a = jnp.dot(h_ref[...], wu_ref[...].astype(BF16), preferred_element_type=F32)
    a = jnp.maximum(a, 0.0)
    a = (a * a).astype(BF16)
    o_ref[...] += jnp.dot(a, wd_ref[...].astype(BF16), preferred_element_type=F32)


def _mlp(x, g, w_up, w_down):
    m = x.shape[0]
    tm = min(ROW_TILE, m)
    return pl.pallas_call(
        _mlp_kernel,
        grid=(m // tm, D_FF // FF_TILE),
        in_specs=[
            pl.BlockSpec((tm, D_MODEL), lambda i, f: (i, 0), pipeline_mode=pl.Buffered(1)),
            pl.BlockSpec((1, D_MODEL), lambda i, f: (0, 0)),
            pl.BlockSpec((D_MODEL, FF_TILE), lambda i, f: (0, f)),
            pl.BlockSpec((FF_TILE, D_MODEL), lambda i, f: (f, 0)),
        ],
        out_specs=pl.BlockSpec((tm, D_MODEL), lambda i, f: (i, 0)),
        out_shape=jax.ShapeDtypeStruct((m, D_MODEL), F32),
        scratch_shapes=[pltpu.VMEM((tm, D_MODEL), BF16)],
        compiler_params=_params("parallel", "arbitrary"),
        name="mlp",
    )(x, g.reshape(1, D_MODEL), w_up, w_down)


def _glu_kernel(x_ref, g_ref, wa_ref, wg_ref, ba_ref, bg_ref, u_ref, h_ref):
    @pl.when(pl.program_id(1) == 0)
    def _():
        h_ref[...] = _rms_rows(x_ref[...], g_ref[...]).astype(BF16)

    h = h_ref[...]
    a = jnp.dot(h, wa_ref[...].astype(BF16), preferred_element_type=F32) + ba_ref[...]
    gate = jnp.dot(h, wg_ref[...].astype(BF16), preferred_element_type=F32) + bg_ref[...]
    u_ref[...] = a * jax.nn.sigmoid(gate)


def _glu(x, g, w_in, b_in):
    m = x.shape[0]
    tm = min(ROW_TILE, m)
    nj = D_MODEL // COL_TILE
    b2 = b_in.reshape(1, 2 * D_MODEL)
    return pl.pallas_call(
        _glu_kernel,
        grid=(m // tm, nj),
        in_specs=[
            pl.BlockSpec((tm, D_MODEL), lambda i, j: (i, 0)),
            pl.BlockSpec((1, D_MODEL), lambda i, j: (0, 0)),
            pl.BlockSpec((D_MODEL, COL_TILE), lambda i, j: (0, j)),
            pl.BlockSpec((D_MODEL, COL_TILE), lambda i, j: (0, j + nj)),
            pl.BlockSpec((1, COL_TILE), lambda i, j: (0, j)),
            pl.BlockSpec((1, COL_TILE), lambda i, j: (0, j + nj)),
        ],
        out_specs=pl.BlockSpec((tm, COL_TILE), lambda i, j: (i, j)),
        out_shape=jax.ShapeDtypeStruct((m, D_MODEL), F32),
        scratch_shapes=[pltpu.VMEM((tm, D_MODEL), BF16)],
        compiler_params=_params("parallel", "arbitrary"),
        name="glu",
    )(x, g.reshape(1, D_MODEL), w_in, w_in, b2, b2)


def _ln_swish(c, g, b):
    mu = jnp.mean(c, axis=-1, keepdims=True)
    xc = c - mu
    var = jnp.mean(xc * xc, axis=-1, keepdims=True)
    y = xc * lax.rsqrt(var + LN_EPS) * g + b
    return y * jax.nn.sigmoid(y)


def _conv_prompt_kernel(cur_ref, halo_ref, w_ref, bdw_ref, g_ref, b_ref, o_ref, win_ref, c_ref):
    t = pl.program_id(1)
    halo = halo_ref[...]
    win_ref[0:CONV_HALO, :] = jnp.where(t > 0, halo, jnp.zeros_like(halo))
    win_ref[CONV_HALO:, :] = cur_ref[...]
    off = CONV_HALO - (CONV_WIDTH - 1)
    for c in range(D_MODEL // LANES):
        cs = slice(c * LANES, (c + 1) * LANES)
        acc = jnp.zeros((CONV_ROWS, LANES), F32)
        for j in range(CONV_WIDTH):
            acc = acc + win_ref[off + j:off + j + CONV_ROWS, cs] * w_ref[j:j + 1, cs]
        c_ref[:, cs] = acc + bdw_ref[:, cs]
    o_ref[...] = _ln_swish(c_ref[...], g_ref[...], b_ref[...]).astype(BF16)


def _conv_prompt(u, w_dw, b_dw, ln_g, ln_b, batch, seq):
    nt = seq // CONV_ROWS
    ratio = CONV_ROWS // CONV_HALO
    vec = lambda v: v.reshape(1, D_MODEL)
    full = lambda shape: pl.BlockSpec(shape, lambda b, t: (0, 0))
    return pl.pallas_call(
        _conv_prompt_kernel,
        grid=(batch, nt),
        in_specs=[
            pl.BlockSpec((CONV_ROWS, D_MODEL), lambda b, t: (b * nt + t, 0)),
            pl.BlockSpec((CONV_HALO, D_MODEL),
                         lambda b, t: (jnp.maximum((b * nt + t) * ratio - 1, 0), 0)),
            full((CONV_WIDTH, D_MODEL)),
            full((1, D_MODEL)),
            full((1, D_MODEL)),
            full((1, D_MODEL)),
        ],
        out_specs=pl.BlockSpec((CONV_ROWS, D_MODEL), lambda b, t: (b * nt + t, 0)),
        out_shape=jax.ShapeDtypeStruct((batch * seq, D_MODEL), BF16),
        scratch_shapes=[pltpu.VMEM((CONV_HALO + CONV_ROWS, D_MODEL), F32),
                        pltpu.VMEM((CONV_ROWS, D_MODEL), F32)],
        compiler_params=_params("parallel", "arbitrary"),
        name="conv_prompt",
    )(u, u, w_dw, vec(b_dw), vec(ln_g), vec(ln_b))


CONV_SAMPLE_BATCH_BLOCK = 4


def _conv_sample_kernel(u_ref, w_ref, bdw_ref, g_ref, b_ref, o_ref, *, t_new):
    outs = []
    for bb in range(CONV_SAMPLE_BATCH_BLOCK):
        acc = jnp.zeros((t_new, D_MODEL), F32)
        for j in range(CONV_WIDTH):
            acc = acc + u_ref[bb, j:j + t_new, :] * w_ref[j:j + 1, :]
        outs.append(acc + bdw_ref[...])
    c = jnp.concatenate(outs, axis=0)
    o_ref[...] = _ln_swish(c, g_ref[...], b_ref[...]).astype(BF16)


def _conv_sample(u_pad, w_dw, b_dw, ln_g, ln_b):
    batch, rows, _ = u_pad.shape
    t_new = rows - (CONV_WIDTH - 1)
    vec = lambda v: v.reshape(1, D_MODEL)
    full = lambda shape: pl.BlockSpec(shape, lambda b: (0, 0))
    return pl.pallas_call(
        functools.partial(_conv_sample_kernel, t_new=t_new),
        grid=(batch // CONV_SAMPLE_BATCH_BLOCK,),
        in_specs=[
            pl.BlockSpec((CONV_SAMPLE_BATCH_BLOCK, rows, D_MODEL), lambda b: (b, 0, 0)),
            full((CONV_WIDTH, D_MODEL)),
            full((1, D_MODEL)),
            full((1, D_MODEL)),
            full((1, D_MODEL)),
        ],
        out_specs=pl.BlockSpec((CONV_SAMPLE_BATCH_BLOCK * t_new, D_MODEL), lambda b: (b, 0)),
        out_shape=jax.ShapeDtypeStruct((batch * t_new, D_MODEL), BF16),
        compiler_params=_params("parallel"),
        name="conv_sample",
    )(u_pad, w_dw, vec(b_dw), vec(ln_g), vec(ln_b))


def _rms_kernel(x_ref, g_ref, o_ref):
    o_ref[...] = _rms_rows(x_ref[...], g_ref[...])


def _rms_norm(x, g):
    m = x.shape[0]
    tm = min(512, m)
    return pl.pallas_call(
        _rms_kernel,
        grid=(m // tm,),
        in_specs=[pl.BlockSpec((tm, D_MODEL), lambda i: (i, 0)),
                  pl.BlockSpec((1, D_MODEL), lambda i: (0, 0))],
        out_specs=pl.BlockSpec((tm, D_MODEL), lambda i: (i, 0)),
        out_shape=jax.ShapeDtypeStruct((m, D_MODEL), F32),
        compiler_params=_params("parallel"),
        name="final_norm",
    )(x, g.reshape(1, D_MODEL))


def _rope_tables(pos):
    inv = ROPE_THETA ** (-jnp.arange(0, HEAD_DIM, 2, dtype=F32) / HEAD_DIM)
    ang = pos.astype(F32)[:, None] * inv[None, :]
    cos, sin = jnp.cos(ang), jnp.sin(ang)
    reps = COL_TILE // HEAD_DIM
    cos_t = jnp.tile(jnp.concatenate([cos, cos], axis=-1), (1, reps))
    sin_t = jnp.tile(jnp.concatenate([-sin, sin], axis=-1), (1, reps))
    return cos_t, sin_t


def kernel(x_prompt, x_sample, cache_k, cache_v, state_conv, norm_mix, norm_mlp, norm_final,
           attn_w_qkv, attn_b_qkv, attn_sinks, attn_w_o, attn_b_o,
           conv_w_in, conv_b_in, conv_w_dw, conv_b_dw, conv_ln_g, conv_ln_b,
           conv_w_out, conv_b_out, mlp_w_up, mlp_w_down):
    batch, seq, _ = x_prompt.shape
    dbatch, t_new, _ = x_sample.shape
    xp = x_prompt.reshape(batch * seq, D_MODEL)
    xs = x_sample.reshape(dbatch * t_new, D_MODEL)

    cos_p, sin_p = _rope_tables(jnp.arange(seq, dtype=jnp.int32))
    cos_s, sin_s = _rope_tables(PAST_LEN + jnp.arange(t_new, dtype=jnp.int32))
    cos_s = jnp.tile(cos_s, (dbatch, 1))
    sin_s = jnp.tile(sin_s, (dbatch, 1))

    k_p, v_p, c_p, k_s, v_s, c_s = [], [], [], [], [], []
    for i in range(DEPTH):
        j = i // 2
        if i % 2 == 0:
            w, b, sinks = attn_w_qkv[j], attn_b_qkv[j], attn_sinks[j]
            q, kv = _qkv_rope(xp, norm_mix[i], w, b, cos_p, sin_p)
            o = _attn_prompt(q, kv, sinks, batch, seq)
            xp = _proj_res(o, attn_w_o[j], attn_b_o[j], xp)
            kv3 = kv.reshape(batch, seq, 2 * KV_DIM)[:, seq - WINDOW:]
            k_p.append(kv3[..., :KV_DIM].reshape(batch, WINDOW, N_KV_HEADS, HEAD_DIM))
            v_p.append(kv3[..., KV_DIM:].reshape(batch, WINDOW, N_KV_HEADS, HEAD_DIM))
            q, kv = _qkv_rope(xs, norm_mix[i], w, b, cos_s, sin_s)
            ck = cache_k[j].reshape(dbatch, WINDOW, KV_DIM)
            cv = cache_v[j].reshape(dbatch, WINDOW, KV_DIM)
            o = _attn_sample(q, kv, ck, cv, sinks, dbatch, t_new)
            xs = _proj_res(o, attn_w_o[j], attn_b_o[j], xs)
            kv3 = kv.reshape(dbatch, t_new, 2 * KV_DIM)
            k_new = kv3[..., :KV_DIM].reshape(dbatch, t_new, N_KV_HEADS, HEAD_DIM)
            v_new = kv3[..., KV_DIM:].reshape(dbatch, t_new, N_KV_HEADS, HEAD_DIM)
            k_s.append(jnp.concatenate([cache_k[j][:, t_new:], k_new], axis=1))
            v_s.append(jnp.concatenate([cache_v[j][:, t_new:], v_new], axis=1))
        else:
            conv_args = (conv_w_dw[j], conv_b_dw[j], conv_ln_g[j], conv_ln_b[j])
            u = _glu(xp, norm_mix[i], conv_w_in[j], conv_b_in[j])
            c = _conv_prompt(u, *conv_args, batch, seq)
            xp = _proj_res(c, conv_w_out[j], conv_b_out[j], xp)
            c_p.append(u.reshape(batch, seq, D_MODEL)[:, seq - (CONV_WIDTH - 1):])
            u = _glu(xs, norm_mix[i], conv_w_in[j], conv_b_in[j])
            u_pad = jnp.concatenate([state_conv[j], u.reshape(dbatch, t_new, D_MODEL)], axis=1)
            c = _conv_sample(u_pad, *conv_args)
            xs = _proj_res(c, conv_w_out[j], conv_b_out[j], xs)
            c_s.append(u_pad[:, t_new:])
        xp = _mlp(xp, norm_mlp[i], mlp_w_up[i], mlp_w_down[i])
        xs = _mlp(xs, norm_mlp[i], mlp_w_up[i], mlp_w_down[i])

    y_p = _rms_norm(xp, norm_final).reshape(batch, seq, D_MODEL)
    y_s = _rms_norm(xs, norm_final).reshape(dbatch, t_new, D_MODEL)
    return (y_p, y_s, jnp.stack(k_p), jnp.stack(v_p), jnp.stack(c_p),
            jnp.stack(k_s), jnp.stack(v_s), jnp.stack(c_s))
```

```python
import functools

import jax
import jax.numpy as jnp
from jax import lax
from jax.experimental import pallas as pl
from jax.experimental.pallas import tpu as pltpu

D_MODEL = 2048
HEAD_DIM = 64
N_HEADS = 32
N_KV_HEADS = 8
GROUP = N_HEADS // N_KV_HEADS
Q_DIM = N_HEADS * HEAD_DIM
KV_DIM = N_KV_HEADS * HEAD_DIM
WINDOW = 128
PAST_LEN = 16384
ROPE_THETA = 10000.0
CONV_WIDTH = 31
D_FF = 4 * D_MODEL
RMS_EPS = 1e-6
LN_EPS = 1e-5
DEPTH = 4

VMEM_LIMIT_BYTES = 60 * 1024 * 1024
LANES = 128
SUBLANES = 8
ROW_TILE = 1024
COL_TILE = 512
FF_TILE = 512
CONV_ROWS = 128
CONV_HALO = 32
HEAD_SLAB = GROUP * HEAD_DIM

assert COL_TILE == KV_DIM and LANES == 2 * HEAD_DIM and HEAD_SLAB == 2 * LANES

BF16 = jnp.bfloat16
F32 = jnp.float32
NT_DIMS = (((1,), (1,)), ((), ()))


def _params(*sem):
    return pltpu.CompilerParams(dimension_semantics=sem, vmem_limit_bytes=VMEM_LIMIT_BYTES)


def _rms_rows(x, g):
    ms = jnp.mean(x * x, axis=-1, keepdims=True)
    return x * lax.rsqrt(ms + RMS_EPS) * g


def _layer_vec(v):
    return v.reshape(v.shape[0], 1, v.shape[1])


def _qkv_kernel(x_ref, g_ref, w_ref, b_ref, cos_ref, sin_ref, q_ref, kv_ref, h_ref):
    j = pl.program_id(1)
    nq = Q_DIM // COL_TILE

    @pl.when(j == 0)
    def _():
        h_ref[...] = _rms_rows(x_ref[...], g_ref[...]).astype(BF16)

    y = jnp.dot(h_ref[...], w_ref[...].astype(BF16), preferred_element_type=F32) + b_ref[...]
    lane = lax.broadcasted_iota(jnp.int32, y.shape, 1)
    first_half = (lane % HEAD_DIM) < (HEAD_DIM // 2)
    rot = jnp.where(first_half,
                    pltpu.roll(y, COL_TILE - HEAD_DIM // 2, 1),
                    pltpu.roll(y, HEAD_DIM // 2, 1))
    r = y * cos_ref[...] + rot * sin_ref[...]

    @pl.when(j < nq)
    def _():
        q_ref[...] = r.astype(BF16)

    @pl.when(j >= nq)
    def _():
        kv_ref[...] = r


def _qkv_rope(x, g, w, b, cos_t, sin_t, layer, norm_layer):
    m = x.shape[0]
    tm = min(ROW_TILE, m)
    n_tab = cos_t.shape[0] // tm
    nq = Q_DIM // COL_TILE

    def tab_map(i, j):
        return (i % n_tab, jnp.maximum(j - nq + 1, 0))

    return pl.pallas_call(
        _qkv_kernel,
        grid=(m // tm, (Q_DIM + 2 * KV_DIM) // COL_TILE),
        in_specs=[
            pl.BlockSpec((tm, D_MODEL), lambda i, j: (i, 0)),
            pl.BlockSpec((None, 1, D_MODEL), lambda i, j: (norm_layer, 0, 0)),
            pl.BlockSpec((None, D_MODEL, COL_TILE), lambda i, j: (layer, 0, j)),
            pl.BlockSpec((None, 1, COL_TILE), lambda i, j: (layer, 0, j)),
            pl.BlockSpec((tm, COL_TILE), tab_map),
            pl.BlockSpec((tm, COL_TILE), tab_map),
        ],
        out_specs=[
            pl.BlockSpec((tm, COL_TILE), lambda i, j: (i, jnp.minimum(j, nq - 1))),
            pl.BlockSpec((tm, COL_TILE), lambda i, j: (i, jnp.maximum(j - nq, 0))),
        ],
        out_shape=[
            jax.ShapeDtypeStruct((m, Q_DIM), BF16),
            jax.ShapeDtypeStruct((m, 2 * KV_DIM), F32),
        ],
        scratch_shapes=[pltpu.VMEM((tm, D_MODEL), BF16)],
        compiler_params=_params("parallel", "arbitrary"),
        name="qkv_rope",
    )(x, g, w, b, cos_t, sin_t)


def _rep_heads(x):
    n = x.shape[1]
    lane = lax.broadcasted_iota(jnp.int32, x.shape, 1)
    low = (lane % LANES) < HEAD_DIM
    even = jnp.where(low, x, pltpu.roll(x, HEAD_DIM, 1))
    odd = jnp.where(low, pltpu.roll(x, n - HEAD_DIM, 1), x)
    pieces = []
    for s in range(n // LANES):
        sl = slice(s * LANES, (s + 1) * LANES)
        pieces += [even[:, sl], even[:, sl], odd[:, sl], odd[:, sl]]
    return jnp.concatenate(pieces, axis=1)


def _attend(qh, k_rep, v_rep, bias, sink, lane_group):
    t = qh.shape[0]
    zero = jnp.zeros_like(qh)
    qm = jnp.concatenate([jnp.where(lane_group == g, qh, zero) for g in range(GROUP)], axis=0)
    s = lax.dot_general(qm, k_rep, NT_DIMS, preferred_element_type=F32) + bias
    mx = jnp.maximum(jnp.max(s, axis=-1, keepdims=True), sink)
    p = jnp.exp(s - mx)
    denom = jnp.sum(p, axis=-1, keepdims=True) + jnp.exp(sink - mx)
    pv = jnp.dot(p.astype(BF16), v_rep, preferred_element_type=F32) * (1.0 / denom)
    out = pv[(GROUP - 1) * t:]
    for g in range(GROUP - 2, -1, -1):
        out = jnp.where(lane_group == g, pv[g * t:(g + 1) * t], out)
    return out


def _sink_rows(sinks_ref, layer, h, t):
    return jnp.concatenate(
        [jnp.full((t, 1), sinks_ref[layer, h * GROUP + g], F32) for g in range(GROUP)], axis=0)


def _attn_prompt_kernel(sinks_ref, q_ref, kc_ref, vc_ref, o_ref, kb_ref, vt_ref, ot_ref, *, layer):
    n = pl.program_id(1)

    @pl.when(n == 0)
    def _():
        kb_ref[0:WINDOW, :] = jnp.zeros((WINDOW, Q_DIM), BF16)
        vt_ref[:, 0:WINDOW] = jnp.zeros((KV_DIM, WINDOW), BF16)

    @pl.when(n > 0)
    def _():
        kb_ref[0:WINDOW, :] = kb_ref[WINDOW:, :]
        vt_ref[:, 0:WINDOW] = vt_ref[:, WINDOW:]

    kb_ref[WINDOW:, :] = _rep_heads(kc_ref[...]).astype(BF16)
    vt_ref[:, WINDOW:] = vc_ref[...].T.astype(BF16)

    cols = GROUP * WINDOW
    key = lax.broadcasted_iota(jnp.int32, (2 * WINDOW, cols), 0)
    qi = lax.broadcasted_iota(jnp.int32, (2 * WINDOW, cols), 1) % WINDOW
    allowed = jnp.logical_or(
        jnp.logical_and(key < WINDOW, jnp.logical_and(key >= qi, n > 0)),
        jnp.logical_and(key >= WINDOW, key - WINDOW <= qi))
    bias = jnp.where(allowed, 0.0, -jnp.inf).astype(F32)
    lane_group = lax.broadcasted_iota(jnp.int32, (WINDOW, HEAD_SLAB), 1) // HEAD_DIM
    col_group = lax.broadcasted_iota(jnp.int32, (1, cols), 1) // WINDOW
    for h in range(N_KV_HEADS):
        hs = slice(h * HEAD_SLAB, (h + 1) * HEAD_SLAB)
        qh = q_ref[:, hs]
        zero = jnp.zeros_like(qh)
        qm = jnp.concatenate([jnp.where(lane_group == g, qh, zero) for g in range(GROUP)], axis=0)
        s = lax.dot_general(kb_ref[:, hs], qm, NT_DIMS, preferred_element_type=F32) + bias
        sink = jnp.full((1, cols), sinks_ref[layer, h * GROUP + GROUP - 1], F32)
        for g in range(GROUP - 2, -1, -1):
            sink = jnp.where(col_group == g, sinks_ref[layer, h * GROUP + g], sink)
        mx = jnp.maximum(jnp.max(s, axis=0, keepdims=True), sink)
        p = jnp.exp(s - mx)
        denom = jnp.sum(p, axis=0, keepdims=True) + jnp.exp(sink - mx)
        ot = jnp.dot(vt_ref[h * HEAD_DIM:(h + 1) * HEAD_DIM, :], p.astype(BF16),
                     preferred_element_type=F32) * (1.0 / denom)
        for g in range(GROUP):
            r0 = h * HEAD_SLAB + g * HEAD_DIM
            ot_ref[r0:r0 + HEAD_DIM, :] = ot[:, g * WINDOW:(g + 1) * WINDOW]
    o_ref[...] = ot_ref[...].T.astype(BF16)


def _attn_prompt(q, kv, sinks, batch, seq, layer):
    nb = seq // WINDOW
    return pl.pallas_call(
        functools.partial(_attn_prompt_kernel, layer=layer),
        grid=(batch, nb),
        in_specs=[
            pl.BlockSpec(memory_space=pltpu.SMEM),
            pl.BlockSpec((WINDOW, Q_DIM), lambda b, n: (b * nb + n, 0)),
            pl.BlockSpec((WINDOW, KV_DIM), lambda b, n: (b * nb + n, 0)),
            pl.BlockSpec((WINDOW, KV_DIM), lambda b, n: (b * nb + n, 1)),
        ],
        out_specs=pl.BlockSpec((WINDOW, Q_DIM), lambda b, n: (b * nb + n, 0)),
        out_shape=jax.ShapeDtypeStruct((batch * seq, Q_DIM), BF16),
        scratch_shapes=[pltpu.VMEM((2 * WINDOW, Q_DIM), BF16),
                        pltpu.VMEM((KV_DIM, 2 * WINDOW), BF16),
                        pltpu.VMEM((Q_DIM, WINDOW), F32)],
        compiler_params=_params("parallel", "arbitrary"),
        name="attn_prompt",
    )(sinks, q, kv, kv)


SAMPLE_BATCH_BLOCK = 4


def _attn_sample_kernel(sinks_ref, q_ref, kvn_ref, ck_ref, cv_ref, o_ref, *, t_new, layer):
    pad = 2 * SUBLANES - t_new
    n_keys = WINDOW + t_new + pad
    rows = GROUP * t_new
    t_of_row = lax.broadcasted_iota(jnp.int32, (rows, n_keys), 0) % t_new
    col = lax.broadcasted_iota(jnp.int32, (rows, n_keys), 1)
    allowed = jnp.logical_or(
        jnp.logical_and(col < WINDOW, col >= t_of_row),
        jnp.logical_and(col >= WINDOW, col - WINDOW <= t_of_row))
    bias = jnp.where(allowed, 0.0, -jnp.inf).astype(F32)
    lane_group = lax.broadcasted_iota(jnp.int32, (t_new, HEAD_SLAB), 1) // HEAD_DIM
    q_all = q_ref[...].astype(F32)
    kvn = kvn_ref[...]
    zpad = jnp.zeros((pad, KV_DIM), F32)
    for bb in range(SAMPLE_BATCH_BLOCK):
        r0 = bb * t_new
        k_new = jnp.concatenate([kvn[r0:r0 + t_new, :KV_DIM], zpad], axis=0)
        v_new = jnp.concatenate([kvn[r0:r0 + t_new, KV_DIM:], zpad], axis=0)
        k_all = jnp.concatenate([_rep_heads(ck_ref[bb]).astype(BF16),
                                 _rep_heads(k_new).astype(BF16)], axis=0)
        v_all = jnp.concatenate([_rep_heads(cv_ref[bb]).astype(BF16),
                                 _rep_heads(v_new).astype(BF16)], axis=0)
        for h in range(N_KV_HEADS):
            hs = slice(h * HEAD_SLAB, (h + 1) * HEAD_SLAB)
            qh = q_all[r0:r0 + t_new, hs].astype(BF16)
            o_ref[r0:r0 + t_new, hs] = _attend(qh, k_all[:, hs], v_all[:, hs], bias,
                                               _sink_rows(sinks_ref, layer, h, t_new), lane_group)


def _attn_sample(q, kv, cache_k, cache_v, sinks, batch, t_new, layer):
    rows = SAMPLE_BATCH_BLOCK * t_new
    cache_spec = pl.BlockSpec((None, SAMPLE_BATCH_BLOCK, WINDOW, KV_DIM), lambda b: (layer, b, 0, 0))
    return pl.pallas_call(
        functools.partial(_attn_sample_kernel, t_new=t_new, layer=layer),
        grid=(batch // SAMPLE_BATCH_BLOCK,),
        in_specs=[
            pl.BlockSpec(memory_space=pltpu.SMEM),
            pl.BlockSpec((rows, Q_DIM), lambda b: (b, 0)),
            pl.BlockSpec((rows, 2 * KV_DIM), lambda b: (b, 0)),
            cache_spec,
            cache_spec,
        ],
        out_specs=pl.BlockSpec((rows, Q_DIM), lambda b: (b, 0)),
        out_shape=jax.ShapeDtypeStruct((batch * t_new, Q_DIM), F32),
        compiler_params=_params("parallel"),
        name="attn_sample",
    )(sinks, q, kv, cache_k, cache_v)


def _proj_res_kernel(a_ref, w_ref, b_ref, res_ref, o_ref):
    y = jnp.dot(a_ref[...].astype(BF16), w_ref[...].astype(BF16), preferred_element_type=F32)
    o_ref[...] = res_ref[...] + y + b_ref[...]


def _proj_res(a, w, b, res, layer):
    m, k = a.shape
    n = w.shape[-1]
    tm = min(ROW_TILE, m)
    return pl.pallas_call(
        _proj_res_kernel,
        grid=(m // tm, n // COL_TILE),
        in_specs=[
            pl.BlockSpec((tm, k), lambda i, j: (i, 0)),
            pl.BlockSpec((None, k, COL_TILE), lambda i, j: (layer, 0, j)),
            pl.BlockSpec((None, 1, COL_TILE), lambda i, j: (layer, 0, j)),
            pl.BlockSpec((tm, COL_TILE), lambda i, j: (i, j)),
        ],
        out_specs=pl.BlockSpec((tm, COL_TILE), lambda i, j: (i, j)),
        out_shape=jax.ShapeDtypeStruct((m, n), F32),
        compiler_params=_params("parallel", "arbitrary"),
        name="proj_res",
    )(a, w, b, res)


def _mlp_kernel(x_ref, g_ref, wu_ref, wd_ref, o_ref, h_ref):
    f = pl.program_id(1)

    @pl.when(f == 0)
    def _():
        x = x_ref[...]
        h_ref[...] = _rms_rows(x, g_ref[...]).astype(BF16)
        o_ref[...] = x

    a = jnp.dot(h_ref[...], wu_ref[...].astype(BF16), preferred_element_type=F32)
    a = jnp.maximum(a, 0.0)
    a = (a * a).astype(BF16)
    o_ref[...] += jnp.dot(a, wd_ref[...].astype(BF16), preferred_element_type=F32)


def _mlp(x, g, w_up, w_down, layer):
    m = x.shape[0]
    tm = min(ROW_TILE, m)
    return pl.pallas_call(
        _mlp_kernel,
        grid=(m // tm, D_FF // FF_TILE),
        in_specs=[
            pl.BlockSpec((tm, D_MODEL), lambda i, f: (i, 0), pipeline_mode=pl.Buffered(1)),
            pl.BlockSpec((None, 1, D_MODEL), lambda i, f: (layer, 0, 0)),
            pl.BlockSpec((None, D_MODEL, FF_TILE), lambda i, f: (layer, 0, f)),
            pl.BlockSpec((None, FF_TILE, D_MODEL), lambda i, f: (layer, f, 0)),
        ],
        out_specs=pl.BlockSpec((tm, D_MODEL), lambda i, f: (i, 0)),
        out_shape=jax.ShapeDtypeStruct((m, D_MODEL), F32),
        scratch_shapes=[pltpu.VMEM((tm, D_MODEL), BF16)],
        compiler_params=_params("parallel", "arbitrary"),
        name="mlp",
    )(x, g, w_up, w_down)


def _glu_kernel(x_ref, g_ref, wa_ref, wg_ref, ba_ref, bg_ref, u_ref, h_ref):
    @pl.when(pl.program_id(1) == 0)
    def _():
        h_ref[...] = _rms_rows(x_ref[...], g_ref[...]).astype(BF16)

    h = h_ref[...]
    a = jnp.dot(h, wa_ref[...].astype(BF16), preferred_element_type=F32) + ba_ref[...]
    gate = jnp.dot(h, wg_ref[...].astype(BF16), preferred_element_type=F32) + bg_ref[...]
    u_ref[...] = a * jax.nn.sigmoid(gate)


def _glu(x, g, w_in, b_in, layer, norm_layer):
    m = x.shape[0]
    tm = min(ROW_TILE, m)
    nj = D_MODEL // COL_TILE
    return pl.pallas_call(
        _glu_kernel,
        grid=(m // tm, nj),
        in_specs=[
            pl.BlockSpec((tm, D_MODEL), lambda i, j: (i, 0)),
            pl.BlockSpec((None, 1, D_MODEL), lambda i, j: (norm_layer, 0, 0)),
            pl.BlockSpec((None, D_MODEL, COL_TILE), lambda i, j: (layer, 0, j)),
            pl.BlockSpec((None, D_MODEL, COL_TILE), lambda i, j: (layer, 0, j + nj)),
            pl.BlockSpec((None, 1, COL_TILE), lambda i, j: (layer, 0, j)),
            pl.BlockSpec((None, 1, COL_TILE), lambda i, j: (layer, 0, j + nj)),
        ],
        out_specs=pl.BlockSpec((tm, COL_TILE), lambda i, j: (i, j)),
        out_shape=jax.ShapeDtypeStruct((m, D_MODEL), F32),
        scratch_shapes=[pltpu.VMEM((tm, D_MODEL), BF16)],
        compiler_params=_params("parallel", "arbitrary"),
        name="glu",
    )(x, g, w_in, w_in, b_in, b_in)


def _ln_swish(c, g, b):
    mu = jnp.mean(c, axis=-1, keepdims=True)
    xc = c - mu
    var = jnp.mean(xc * xc, axis=-1, keepdims=True)
    y = xc * lax.rsqrt(var + LN_EPS) * g + b
    return y * jax.nn.sigmoid(y)


def _conv_prompt_kernel(cur_ref, halo_ref, w_ref, bdw_ref, g_ref, b_ref, o_ref, win_ref, c_ref):
    t = pl.program_id(1)
    halo = halo_ref[...]
    win_ref[0:CONV_HALO, :] = jnp.where(t > 0, halo, jnp.zeros_like(halo))
    win_ref[CONV_HALO:, :] = cur_ref[...]
    off = CONV_HALO - (CONV_WIDTH - 1)
    n_win = CONV_HALO + CONV_ROWS
    for c in range(D_MODEL // LANES):
        cs = slice(c * LANES, (c + 1) * LANES)
        win = win_ref[:, cs]
        acc = jnp.zeros((CONV_ROWS, LANES), F32)
        for phase in range(SUBLANES):
            taps = [j for j in range(CONV_WIDTH) if (off + j) % SUBLANES == phase]
            if not taps:
                continue
            shifted = win if phase == 0 else pltpu.roll(win, n_win - phase, 0)
            for j in taps:
                base = (off + j) - phase
                acc = acc + shifted[base:base + CONV_ROWS] * w_ref[j:j + 1, cs]
        c_ref[:, cs] = acc + bdw_ref[:, cs]
    o_ref[...] = _ln_swish(c_ref[...], g_ref[...], b_ref[...]).astype(BF16)


def _conv_prompt(u, w_dw, b_dw, ln_g, ln_b, batch, seq, layer):
    nt = seq // CONV_ROWS
    ratio = CONV_ROWS // CONV_HALO
    vec = pl.BlockSpec((None, 1, D_MODEL), lambda b, t: (layer, 0, 0))
    return pl.pallas_call(
        _conv_prompt_kernel,
        grid=(batch, nt),
        in_specs=[
            pl.BlockSpec((CONV_ROWS, D_MODEL), lambda b, t: (b * nt + t, 0)),
            pl.BlockSpec((CONV_HALO, D_MODEL),
                         lambda b, t: (jnp.maximum((b * nt + t) * ratio - 1, 0), 0)),
            pl.BlockSpec((None, CONV_WIDTH, D_MODEL), lambda b, t: (layer, 0, 0)),
            vec, vec, vec,
        ],
        out_specs=pl.BlockSpec((CONV_ROWS, D_MODEL), lambda b, t: (b * nt + t, 0)),
        out_shape=jax.ShapeDtypeStruct((batch * seq, D_MODEL), BF16),
        scratch_shapes=[pltpu.VMEM((CONV_HALO + CONV_ROWS, D_MODEL), F32),
                        pltpu.VMEM((CONV_ROWS, D_MODEL), F32)],
        compiler_params=_params("parallel", "arbitrary"),
        name="conv_prompt",
    )(u, u, w_dw, b_dw, ln_g, ln_b)


CONV_SAMPLE_BATCH_BLOCK = 4


def _conv_sample_kernel(u_ref, w_ref, bdw_ref, g_ref, b_ref, o_ref, *, t_new):
    outs = []
    for bb in range(CONV_SAMPLE_BATCH_BLOCK):
        acc = jnp.zeros((t_new, D_MODEL), F32)
        for j in range(CONV_WIDTH):
            acc = acc + u_ref[bb, j:j + t_new, :] * w_ref[j:j + 1, :]
        outs.append(acc + bdw_ref[...])
    c = jnp.concatenate(outs, axis=0)
    o_ref[...] = _ln_swish(c, g_ref[...], b_ref[...]).astype(BF16)


def _conv_sample(u_pad, w_dw, b_dw, ln_g, ln_b, layer):
    batch, rows, _ = u_pad.shape
    t_new = rows - (CONV_WIDTH - 1)
    vec = pl.BlockSpec((None, 1, D_MODEL), lambda b: (layer, 0, 0))
    return pl.pallas_call(
        functools.partial(_conv_sample_kernel, t_new=t_new),
        grid=(batch // CONV_SAMPLE_BATCH_BLOCK,),
        in_specs=[
            pl.BlockSpec((CONV_SAMPLE_BATCH_BLOCK, rows, D_MODEL), lambda b: (b, 0, 0)),
            pl.BlockSpec((None, CONV_WIDTH, D_MODEL), lambda b: (layer, 0, 0)),
            vec, vec, vec,
        ],
        out_specs=pl.BlockSpec((CONV_SAMPLE_BATCH_BLOCK * t_new, D_MODEL), lambda b: (b, 0)),
        out_shape=jax.ShapeDtypeStruct((batch * t_new, D_MODEL), BF16),
        compiler_params=_params("parallel"),
        name="conv_sample",
    )(u_pad, w_dw, b_dw, ln_g, ln_b)


def _rms_kernel(x_ref, g_ref, o_ref):
    o_ref[...] = _rms_rows(x_ref[...], g_ref[...])


def _rms_norm(x, g):
    m = x.shape[0]
    tm = min(512, m)
    return pl.pallas_call(
        _rms_kernel,
        grid=(m // tm,),
        in_specs=[pl.BlockSpec((tm, D_MODEL), lambda i: (i, 0)),
                  pl.BlockSpec((1, D_MODEL), lambda i: (0, 0))],
        out_specs=pl.BlockSpec((tm, D_MODEL), lambda i: (i, 0)),
        out_shape=jax.ShapeDtypeStruct((m, D_MODEL), F32),
        compiler_params=_params("parallel"),
        name="final_norm",
    )(x, g.reshape(1, D_MODEL))


def _rope_tables(pos):
    inv = ROPE_THETA ** (-jnp.arange(0, HEAD_DIM, 2, dtype=F32) / HEAD_DIM)
    ang = pos.astype(F32)[:, None] * inv[None, :]
    cos, sin = jnp.cos(ang), jnp.sin(ang)
    reps = COL_TILE // HEAD_DIM
    cos_t = jnp.tile(jnp.concatenate([cos, cos], axis=-1), (1, reps))
    sin_t = jnp.tile(jnp.concatenate([-sin, sin], axis=-1), (1, reps))
    scale = HEAD_DIM ** -0.5
    cos_all = jnp.concatenate([cos_t * scale, cos_t, jnp.ones_like(cos_t)], axis=1)
    sin_all = jnp.concatenate([sin_t * scale, sin_t, jnp.zeros_like(sin_t)], axis=1)
    return cos_all, sin_all


def kernel(x_prompt, x_sample, cache_k, cache_v, state_conv, norm_mix, norm_mlp, norm_final,
           attn_w_qkv, attn_b_qkv, attn_sinks, attn_w_o, attn_b_o,
           conv_w_in, conv_b_in, conv_w_dw, conv_b_dw, conv_ln_g, conv_ln_b,
           conv_w_out, conv_b_out, mlp_w_up, mlp_w_down):
    batch, seq, _ = x_prompt.shape
    dbatch, t_new, _ = x_sample.shape
    xp = x_prompt.reshape(batch * seq, D_MODEL)
    xs = x_sample.reshape(dbatch * t_new, D_MODEL)

    cos_p, sin_p = _rope_tables(jnp.arange(seq, dtype=jnp.int32))
    cos_s, sin_s = _rope_tables(PAST_LEN + jnp.arange(t_new, dtype=jnp.int32))
    cos_s = jnp.tile(cos_s, (dbatch, 1))
    sin_s = jnp.tile(sin_s, (dbatch, 1))

    norm_mix, norm_mlp = _layer_vec(norm_mix), _layer_vec(norm_mlp)
    attn_b_qkv, attn_b_o = _layer_vec(attn_b_qkv), _layer_vec(attn_b_o)
    conv_b_in, conv_b_out = _layer_vec(conv_b_in), _layer_vec(conv_b_out)
    conv_b_dw, conv_ln_g, conv_ln_b = _layer_vec(conv_b_dw), _layer_vec(conv_ln_g), _layer_vec(conv_ln_b)
    n_attn = cache_k.shape[0]
    ck = cache_k.reshape(n_attn, dbatch, WINDOW, KV_DIM)
    cv = cache_v.reshape(n_attn, dbatch, WINDOW, KV_DIM)

    k_p, v_p, c_p, k_s, v_s, c_s = [], [], [], [], [], []
    for i in range(DEPTH):
        j = i // 2
        if i % 2 == 0:
            q, kv = _qkv_rope(xp, norm_mix, attn_w_qkv, attn_b_qkv, cos_p, sin_p, j, i)
            o = _attn_prompt(q, kv, attn_sinks, batch, seq, j)
            xp = _proj_res(o, attn_w_o, attn_b_o, xp, j)
            kv3 = kv.reshape(batch, seq, 2 * KV_DIM)[:, seq - WINDOW:]
            k_p.append(kv3[..., :KV_DIM].reshape(batch, WINDOW, N_KV_HEADS, HEAD_DIM))
            v_p.append(kv3[..., KV_DIM:].reshape(batch, WINDOW, N_KV_HEADS, HEAD_DIM))
            q, kv = _qkv_rope(xs, norm_mix, attn_w_qkv, attn_b_qkv, cos_s, sin_s, j, i)
            o = _attn_sample(q, kv, ck, cv, attn_sinks, dbatch, t_new, j)
            xs = _proj_res(o, attn_w_o, attn_b_o, xs, j)
            kv3 = kv.reshape(dbatch, t_new, 2 * KV_DIM)
            k_new = kv3[..., :KV_DIM].reshape(dbatch, t_new, N_KV_HEADS, HEAD_DIM)
            v_new = kv3[..., KV_DIM:].reshape(dbatch, t_new, N_KV_HEADS, HEAD_DIM)
            k_s.append(jnp.concatenate([cache_k[j][:, t_new:], k_new], axis=1))
            v_s.append(jnp.concatenate([cache_v[j][:, t_new:], v_new], axis=1))
        else:
            conv_args = (conv_w_dw, conv_b_dw, conv_ln_g, conv_ln_b)
            u = _glu(xp, norm_mix, conv_w_in, conv_b_in, j, i)
            c = _conv_prompt(u, *conv_args, batch, seq, j)
            xp = _proj_res(c, conv_w_out, conv_b_out, xp, j)
            c_p.append(u.reshape(batch, seq, D_MODEL)[:, seq - (CONV_WIDTH - 1):])
            u = _glu(xs, norm_mix, conv_w_in, conv_b_in, j, i)
            u_pad = jnp.concatenate([state_conv[j], u.reshape(dbatch, t_new, D_MODEL)], axis=1)
            c = _conv_sample(u_pad, *conv_args, j)
            xs = _proj_res(c, conv_w_out, conv_b_out, xs, j)
            c_s.append(u_pad[:, t_new:])
        xp = _mlp(xp, norm_mlp, mlp_w_up, mlp_w_down, i)
        xs = _mlp(xs, norm_mlp, mlp_w_up, mlp_w_down, i)

    y_p = _rms_norm(xp, norm_final).reshape(batch, seq, D_MODEL)
    y_s = _rms_norm(xs, norm_final).reshape(dbatch, t_new, D_MODEL)
    return (y_p, y_s, jnp.stack(k_p), jnp.stack(v_p), jnp.stack(c_p),
            jnp.stack(k_s), jnp.stack(v_s), jnp.stack(c_s))
```

```python
import functools

import jax
import jax.numpy as jnp
from jax import lax
from jax.experimental import pallas as pl
from jax.experimental.pallas import tpu as pltpu

D_MODEL = 2048
HEAD_DIM = 64
N_HEADS = 32
N_KV_HEADS = 8
GROUP = N_HEADS // N_KV_HEADS
Q_DIM = N_HEADS * HEAD_DIM
KV_DIM = N_KV_HEADS * HEAD_DIM
WINDOW = 128
PAST_LEN = 16384
ROPE_THETA = 10000.0
CONV_WIDTH = 31
D_FF = 4 * D_MODEL
RMS_EPS = 1e-6
LN_EPS = 1e-5
DEPTH = 4

VMEM_LIMIT_BYTES = 60 * 1024 * 1024
LANES = 128
SUBLANES = 8
ROW_TILE = 1024
COL_TILE = 512
FF_TILE = 512
CONV_ROWS = 128
CONV_HALO = 32
HEAD_SLAB = GROUP * HEAD_DIM

assert COL_TILE == KV_DIM and LANES == 2 * HEAD_DIM and HEAD_SLAB == 2 * LANES

BF16 = jnp.bfloat16
F32 = jnp.float32
NT_DIMS = (((1,), (1,)), ((), ()))


def _params(*sem):
    return pltpu.CompilerParams(dimension_semantics=sem, vmem_limit_bytes=VMEM_LIMIT_BYTES)


def _rms_rows(x, g):
    ms = jnp.mean(x * x, axis=-1, keepdims=True)
    return x * lax.rsqrt(ms + RMS_EPS) * g


def _layer_vec(v):
    return v.reshape(v.shape[0], 1, v.shape[1])


def _first_pass_tile(layer, n_tiles):
    def index_map(i, j):
        return (layer, 0, jnp.where(i == 0, j, n_tiles - 1))
    return index_map


def _qkv_kernel(x_ref, g_ref, w_ref, b_ref, cos_ref, sin_ref, q_ref, kv_ref, h_ref, wbf_ref):
    i = pl.program_id(0)
    j = pl.program_id(1)
    nq = Q_DIM // COL_TILE

    @pl.when(i == 0)
    def _():
        wbf_ref[j] = w_ref[...].astype(BF16)

    @pl.when(j == 0)
    def _():
        h_ref[...] = _rms_rows(x_ref[...], g_ref[...]).astype(BF16)

    y = jnp.dot(h_ref[...], wbf_ref[j], preferred_element_type=F32) + b_ref[...]
    lane = lax.broadcasted_iota(jnp.int32, y.shape, 1)
    first_half = (lane % HEAD_DIM) < (HEAD_DIM // 2)
    rot = jnp.where(first_half,
                    pltpu.roll(y, COL_TILE - HEAD_DIM // 2, 1),
                    pltpu.roll(y, HEAD_DIM // 2, 1))
    reps = COL_TILE // LANES
    r = y * jnp.tile(cos_ref[...], (1, reps)) + rot * jnp.tile(sin_ref[...], (1, reps))

    @pl.when(j < nq)
    def _():
        q_ref[...] = r.astype(BF16)

    @pl.when(j >= nq)
    def _():
        kv_ref[...] = r


def _qkv_rope(x, g, w, b, cos_t, sin_t, layer, norm_layer):
    m = x.shape[0]
    tm = min(ROW_TILE, m)
    n_tab = cos_t.shape[0] // tm
    nq = Q_DIM // COL_TILE
    nj = (Q_DIM + 2 * KV_DIM) // COL_TILE

    def tab_map(i, j):
        return (i % n_tab, jnp.maximum(j - nq + 1, 0))

    return pl.pallas_call(
        _qkv_kernel,
        grid=(m // tm, nj),
        in_specs=[
            pl.BlockSpec((tm, D_MODEL), lambda i, j: (i, 0), pipeline_mode=pl.Buffered(1)),
            pl.BlockSpec((None, 1, D_MODEL), lambda i, j: (norm_layer, 0, 0)),
            pl.BlockSpec((None, D_MODEL, COL_TILE), _first_pass_tile(layer, nj)),
            pl.BlockSpec((None, 1, COL_TILE), lambda i, j: (layer, 0, j)),
            pl.BlockSpec((tm, LANES), tab_map),
            pl.BlockSpec((tm, LANES), tab_map),
        ],
        out_specs=[
            pl.BlockSpec((tm, COL_TILE), lambda i, j: (i, jnp.minimum(j, nq - 1))),
            pl.BlockSpec((tm, COL_TILE), lambda i, j: (i, jnp.maximum(j - nq, 0))),
        ],
        out_shape=[
            jax.ShapeDtypeStruct((m, Q_DIM), BF16),
            jax.ShapeDtypeStruct((m, 2 * KV_DIM), F32),
        ],
        scratch_shapes=[pltpu.VMEM((tm, D_MODEL), BF16),
                        pltpu.VMEM((nj, D_MODEL, COL_TILE), BF16)],
        compiler_params=_params("arbitrary", "arbitrary"),
        name="qkv_rope",
    )(x, g, w, b, cos_t, sin_t)


def _rep_heads(x):
    n = x.shape[1]
    lane = lax.broadcasted_iota(jnp.int32, x.shape, 1)
    low = (lane % LANES) < HEAD_DIM
    even = jnp.where(low, x, pltpu.roll(x, HEAD_DIM, 1))
    odd = jnp.where(low, pltpu.roll(x, n - HEAD_DIM, 1), x)
    pieces = []
    for s in range(n // LANES):
        sl = slice(s * LANES, (s + 1) * LANES)
        pieces += [even[:, sl], even[:, sl], odd[:, sl], odd[:, sl]]
    return jnp.concatenate(pieces, axis=1)


def _attn_prompt_kernel(sinks_ref, q_ref, kc_ref, vc_ref, o_ref, kb_ref, vt_ref, ot_ref, *, layer):
    n = pl.program_id(1)

    @pl.when(n == 0)
    def _():
        kb_ref[0:WINDOW, :] = jnp.zeros((WINDOW, Q_DIM), BF16)
        vt_ref[:, 0:WINDOW] = jnp.zeros((KV_DIM, WINDOW), BF16)

    @pl.when(n > 0)
    def _():
        kb_ref[0:WINDOW, :] = kb_ref[WINDOW:, :]
        vt_ref[:, 0:WINDOW] = vt_ref[:, WINDOW:]

    kb_ref[WINDOW:, :] = _rep_heads(kc_ref[...]).astype(BF16)
    vt_ref[:, WINDOW:] = vc_ref[...].T.astype(BF16)

    cols = GROUP * WINDOW
    key = lax.broadcasted_iota(jnp.int32, (2 * WINDOW, cols), 0)
    qi = lax.broadcasted_iota(jnp.int32, (2 * WINDOW, cols), 1) % WINDOW
    allowed = jnp.logical_or(
        jnp.logical_and(key < WINDOW, jnp.logical_and(key >= qi, n > 0)),
        jnp.logical_and(key >= WINDOW, key - WINDOW <= qi))
    bias = jnp.where(allowed, 0.0, -jnp.inf).astype(F32)
    lane_group = lax.broadcasted_iota(jnp.int32, (WINDOW, HEAD_SLAB), 1) // HEAD_DIM
    col_group = lax.broadcasted_iota(jnp.int32, (1, cols), 1) // WINDOW

    def scores(h):
        hs = slice(h * HEAD_SLAB, (h + 1) * HEAD_SLAB)
        qh = q_ref[:, hs]
        zero = jnp.zeros_like(qh)
        qm = jnp.concatenate([jnp.where(lane_group == g, qh, zero) for g in range(GROUP)], axis=0)
        return lax.dot_general(kb_ref[:, hs], qm, NT_DIMS, preferred_element_type=F32) + bias

    def sink_row(h):
        sink = jnp.full((1, cols), sinks_ref[layer, h * GROUP + GROUP - 1], F32)
        for g in range(GROUP - 2, -1, -1):
            sink = jnp.where(col_group == g, sinks_ref[layer, h * GROUP + g], sink)
        return sink

    def softmax(s, sink):
        mx = jnp.maximum(jnp.max(s, axis=0, keepdims=True), sink)
        p = jnp.exp(s - mx)
        denom = jnp.sum(p, axis=0, keepdims=True) + jnp.exp(sink - mx)
        return p.astype(BF16), 1.0 / denom

    def finish(h, p, inv):
        ot = jnp.dot(vt_ref[h * HEAD_DIM:(h + 1) * HEAD_DIM, :], p, preferred_element_type=F32) * inv
        for g in range(GROUP):
            r0 = h * HEAD_SLAB + g * HEAD_DIM
            ot_ref[r0:r0 + HEAD_DIM, :] = ot[:, g * WINDOW:(g + 1) * WINDOW]

    s_next = scores(0)
    pending = None
    for h in range(N_KV_HEADS):
        s = s_next
        if h + 1 < N_KV_HEADS:
            s_next = scores(h + 1)
        p, inv = softmax(s, sink_row(h))
        if pending is not None:
            finish(*pending)
        pending = (h, p, inv)
    finish(*pending)
    o_ref[...] = ot_ref[...].T.astype(BF16)


def _attn_prompt(q, kv, sinks, batch, seq, layer):
    nb = seq // WINDOW
    return pl.pallas_call(
        functools.partial(_attn_prompt_kernel, layer=layer),
        grid=(batch, nb),
        in_specs=[
            pl.BlockSpec(memory_space=pltpu.SMEM),
            pl.BlockSpec((WINDOW, Q_DIM), lambda b, n: (b * nb + n, 0)),
            pl.BlockSpec((WINDOW, KV_DIM), lambda b, n: (b * nb + n, 0)),
            pl.BlockSpec((WINDOW, KV_DIM), lambda b, n: (b * nb + n, 1)),
        ],
        out_specs=pl.BlockSpec((WINDOW, Q_DIM), lambda b, n: (b * nb + n, 0)),
        out_shape=jax.ShapeDtypeStruct((batch * seq, Q_DIM), BF16),
        scratch_shapes=[pltpu.VMEM((2 * WINDOW, Q_DIM), BF16),
                        pltpu.VMEM((KV_DIM, 2 * WINDOW), BF16),
                        pltpu.VMEM((Q_DIM, WINDOW), F32)],
        compiler_params=_params("parallel", "arbitrary"),
        name="attn_prompt",
    )(sinks, q, kv, kv)


SAMPLE_BATCH_BLOCK = 4


def _attn_sample_kernel(sink_ref, q_ref, kvn_ref, ck_ref, cv_ref, o_ref, own_ref, *, t_new):
    pad = 2 * SUBLANES - t_new
    n_keys = WINDOW + t_new + pad
    rows = N_HEADS * t_new
    t_of_row = lax.broadcasted_iota(jnp.int32, (rows, n_keys), 0) % t_new
    col = lax.broadcasted_iota(jnp.int32, (rows, n_keys), 1)
    allowed = jnp.logical_or(
        jnp.logical_and(col < WINDOW, col >= t_of_row),
        jnp.logical_and(col >= WINDOW, col - WINDOW <= t_of_row))
    bias = jnp.where(allowed, 0.0, -jnp.inf).astype(F32)
    head_of_row = lax.broadcasted_iota(jnp.int32, (rows, Q_DIM), 0) // t_new
    head_of_lane = lax.broadcasted_iota(jnp.int32, (rows, Q_DIM), 1) // HEAD_DIM
    own_ref[...] = jnp.where(head_of_row == head_of_lane, 1.0, 0.0).astype(F32)
    sink = sink_ref[...]
    q_all = q_ref[...].astype(F32)
    kvn = kvn_ref[...]
    zpad = jnp.zeros((pad, KV_DIM), F32)

    def scores(bb):
        r0 = bb * t_new
        k_new = jnp.concatenate([kvn[r0:r0 + t_new, :KV_DIM], zpad], axis=0)
        k_all = jnp.concatenate([_rep_heads(ck_ref[bb]).astype(BF16),
                                 _rep_heads(k_new).astype(BF16)], axis=0)
        qf = (jnp.tile(q_all[r0:r0 + t_new], (N_HEADS, 1)) * own_ref[...]).astype(BF16)
        return lax.dot_general(qf, k_all, NT_DIMS, preferred_element_type=F32) + bias

    def softmax(s):
        mx = jnp.maximum(jnp.max(s, axis=-1, keepdims=True), sink)
        p = jnp.exp(s - mx)
        denom = jnp.sum(p, axis=-1, keepdims=True) + jnp.exp(sink - mx)
        return (p * (1.0 / denom)).astype(BF16)

    def finish(bb, pn):
        r0 = bb * t_new
        v_new = jnp.concatenate([kvn[r0:r0 + t_new, KV_DIM:], zpad], axis=0)
        v_all = jnp.concatenate([_rep_heads(cv_ref[bb]).astype(BF16),
                                 _rep_heads(v_new).astype(BF16)], axis=0)
        pv = jnp.dot(pn, v_all, preferred_element_type=F32) * own_ref[...]
        o = pv[0:t_new]
        for h in range(1, N_HEADS):
            o = o + pv[h * t_new:(h + 1) * t_new]
        o_ref[r0:r0 + t_new, :] = o.astype(BF16)

    s_next = scores(0)
    pending = None
    for bb in range(SAMPLE_BATCH_BLOCK):
        s = s_next
        if bb + 1 < SAMPLE_BATCH_BLOCK:
            s_next = scores(bb + 1)
        pn = softmax(s)
        if pending is not None:
            finish(*pending)
        pending = (bb, pn)
    finish(*pending)


def _attn_sample(q, kv, cache_k, cache_v, sink_rows, batch, t_new, layer):
    rows = SAMPLE_BATCH_BLOCK * t_new
    cache_spec = pl.BlockSpec((None, SAMPLE_BATCH_BLOCK, WINDOW, KV_DIM), lambda b: (layer, b, 0, 0))
    return pl.pallas_call(
        functools.partial(_attn_sample_kernel, t_new=t_new),
        grid=(batch // SAMPLE_BATCH_BLOCK,),
        in_specs=[
            pl.BlockSpec((None, N_HEADS * t_new, 1), lambda b: (layer, 0, 0)),
            pl.BlockSpec((rows, Q_DIM), lambda b: (b, 0)),
            pl.BlockSpec((rows, 2 * KV_DIM), lambda b: (b, 0)),
            cache_spec,
            cache_spec,
        ],
        out_specs=pl.BlockSpec((rows, Q_DIM), lambda b: (b, 0)),
        out_shape=jax.ShapeDtypeStruct((batch * t_new, Q_DIM), BF16),
        scratch_shapes=[pltpu.VMEM((N_HEADS * t_new, Q_DIM), F32)],
        compiler_params=_params("parallel"),
        name="attn_sample",
    )(sink_rows, q, kv, cache_k, cache_v)


def _proj_res_kernel(a_ref, w_ref, b_ref, res_ref, o_ref, wbf_ref):
    j = pl.program_id(1)

    @pl.when(pl.program_id(0) == 0)
    def _():
        wbf_ref[j] = w_ref[...].astype(BF16)

    y = jnp.dot(a_ref[...], wbf_ref[j], preferred_element_type=F32)
    o_ref[...] = res_ref[...] + y + b_ref[...]


def _proj_res(a, w, b, res, layer):
    m, k = a.shape
    n = w.shape[-1]
    tm = min(ROW_TILE, m)
    nj = n // COL_TILE
    return pl.pallas_call(
        _proj_res_kernel,
        grid=(m // tm, nj),
        in_specs=[
            pl.BlockSpec((tm, k), lambda i, j: (i, 0)),
            pl.BlockSpec((None, k, COL_TILE), _first_pass_tile(layer, nj)),
            pl.BlockSpec((None, 1, COL_TILE), lambda i, j: (layer, 0, j)),
            pl.BlockSpec((tm, COL_TILE), lambda i, j: (i, j)),
        ],
        out_specs=pl.BlockSpec((tm, COL_TILE), lambda i, j: (i, j)),
        out_shape=jax.ShapeDtypeStruct((m, n), F32),
        scratch_shapes=[pltpu.VMEM((nj, k, COL_TILE), BF16)],
        compiler_params=_params("arbitrary", "arbitrary"),
        name="proj_res",
    )(a, w, b, res)


def _mlp_kernel(x_ref, g_ref, wu_ref, wd_ref, gf_ref, o_ref, h_ref, *, final_norm):
    f = pl.program_id(1)

    @pl.when(f == 0)
    def _():
        x = x_ref[...]
        h_ref[...] = _rms_rows(x, g_ref[...]).astype(BF16)
        o_ref[...] = x

    a = jnp.dot(h_ref[...], wu_ref[...].astype(BF16), preferred_element_type=F32)
    a = jnp.maximum(a, 0.0)
    a = (a * a).astype(BF16)
    o_ref[...] += jnp.dot(a, wd_ref[...].astype(BF16), preferred_element_type=F32)

    if final_norm:
        @pl.when(f == pl.num_programs(1) - 1)
        def _():
            o_ref[...] = _rms_rows(o_ref[...], gf_ref[...])


def _mlp(x, g, w_up, w_down, g_final, layer, final_norm):
    m = x.shape[0]
    tm = min(ROW_TILE, m)
    return pl.pallas_call(
        functools.partial(_mlp_kernel, final_norm=final_norm),
        grid=(m // tm, D_FF // FF_TILE),
        in_specs=[
            pl.BlockSpec((tm, D_MODEL), lambda i, f: (i, 0), pipeline_mode=pl.Buffered(1)),
            pl.BlockSpec((None, 1, D_MODEL), lambda i, f: (layer, 0, 0)),
            pl.BlockSpec((None, D_MODEL, FF_TILE), lambda i, f: (layer, 0, f)),
            pl.BlockSpec((None, FF_TILE, D_MODEL), lambda i, f: (layer, f, 0)),
            pl.BlockSpec((1, D_MODEL), lambda i, f: (0, 0)),
        ],
        out_specs=pl.BlockSpec((tm, D_MODEL), lambda i, f: (i, 0)),
        out_shape=jax.ShapeDtypeStruct((m, D_MODEL), F32),
        scratch_shapes=[pltpu.VMEM((tm, D_MODEL), BF16)],
        compiler_params=_params("parallel", "arbitrary"),
        name="mlp",
    )(x, g, w_up, w_down, g_final)


def _glu_kernel(x_ref, g_ref, wa_ref, wg_ref, ba_ref, bg_ref, u_ref, h_ref, wbf_ref):
    j = pl.program_id(1)
    nj = D_MODEL // COL_TILE

    @pl.when(pl.program_id(0) == 0)
    def _():
        wbf_ref[j] = wa_ref[...].astype(BF16)
        wbf_ref[nj + j] = wg_ref[...].astype(BF16)

    @pl.when(j == 0)
    def _():
        h_ref[...] = _rms_rows(x_ref[...], g_ref[...]).astype(BF16)

    h = h_ref[...]
    a = jnp.dot(h, wbf_ref[j], preferred_element_type=F32) + ba_ref[...]
    gate = jnp.dot(h, wbf_ref[nj + j], preferred_element_type=F32) + bg_ref[...]
    u_ref[...] = a * jax.nn.sigmoid(gate)


def _glu(x, g, w_in, b_in, layer, norm_layer):
    m = x.shape[0]
    tm = min(ROW_TILE, m)
    nj = D_MODEL // COL_TILE

    def gate_tile(i, j):
        return (layer, 0, nj + jnp.where(i == 0, j, nj - 1))

    return pl.pallas_call(
        _glu_kernel,
        grid=(m // tm, nj),
        in_specs=[
            pl.BlockSpec((tm, D_MODEL), lambda i, j: (i, 0), pipeline_mode=pl.Buffered(1)),
            pl.BlockSpec((None, 1, D_MODEL), lambda i, j: (norm_layer, 0, 0)),
            pl.BlockSpec((None, D_MODEL, COL_TILE), _first_pass_tile(layer, nj)),
            pl.BlockSpec((None, D_MODEL, COL_TILE), gate_tile),
            pl.BlockSpec((None, 1, COL_TILE), lambda i, j: (layer, 0, j)),
            pl.BlockSpec((None, 1, COL_TILE), lambda i, j: (layer, 0, j + nj)),
        ],
        out_specs=pl.BlockSpec((tm, COL_TILE), lambda i, j: (i, j)),
        out_shape=jax.ShapeDtypeStruct((m, D_MODEL), F32),
        scratch_shapes=[pltpu.VMEM((tm, D_MODEL), BF16),
                        pltpu.VMEM((2 * nj, D_MODEL, COL_TILE), BF16)],
        compiler_params=_params("arbitrary", "arbitrary"),
        name="glu",
    )(x, g, w_in, w_in, b_in, b_in)


def _ln_swish(c, g, b):
    mu = jnp.mean(c, axis=-1, keepdims=True)
    xc = c - mu
    var = jnp.mean(xc * xc, axis=-1, keepdims=True)
    y = xc * lax.rsqrt(var + LN_EPS) * g + b
    return y * jax.nn.sigmoid(y)


def _conv_prompt_kernel(cur_ref, halo_ref, w_ref, bdw_ref, g_ref, b_ref, o_ref, win_ref, c_ref):
    t = pl.program_id(1)
    halo = halo_ref[...]
    win_ref[0:CONV_HALO, :] = jnp.where(t > 0, halo, jnp.zeros_like(halo))
    win_ref[CONV_HALO:, :] = cur_ref[...]
    off = CONV_HALO - (CONV_WIDTH - 1)
    n_win = CONV_HALO + CONV_ROWS
    for c in range(D_MODEL // LANES):
        cs = slice(c * LANES, (c + 1) * LANES)
        win = win_ref[:, cs]
        acc = jnp.zeros((CONV_ROWS, LANES), F32)
        for phase in range(SUBLANES):
            taps = [j for j in range(CONV_WIDTH) if (off + j) % SUBLANES == phase]
            if not taps:
                continue
            shifted = win if phase == 0 else pltpu.roll(win, n_win - phase, 0)
            for j in taps:
                base = (off + j) - phase
                acc = acc + shifted[base:base + CONV_ROWS] * w_ref[j:j + 1, cs]
        c_ref[:, cs] = acc + bdw_ref[:, cs]
    o_ref[...] = _ln_swish(c_ref[...], g_ref[...], b_ref[...]).astype(BF16)


def _conv_prompt(u, w_dw, b_dw, ln_g, ln_b, batch, seq, layer):
    nt = seq // CONV_ROWS
    ratio = CONV_ROWS // CONV_HALO
    vec = pl.BlockSpec((None, 1, D_MODEL), lambda b, t: (layer, 0, 0))
    return pl.pallas_call(
        _conv_prompt_kernel,
        grid=(batch, nt),
        in_specs=[
            pl.BlockSpec((CONV_ROWS, D_MODEL), lambda b, t: (b * nt + t, 0)),
            pl.BlockSpec((CONV_HALO, D_MODEL),
                         lambda b, t: (jnp.maximum((b * nt + t) * ratio - 1, 0), 0)),
            pl.BlockSpec((None, CONV_WIDTH, D_MODEL), lambda b, t: (layer, 0, 0)),
            vec, vec, vec,
        ],
        out_specs=pl.BlockSpec((CONV_ROWS, D_MODEL), lambda b, t: (b * nt + t, 0)),
        out_shape=jax.ShapeDtypeStruct((batch * seq, D_MODEL), BF16),
        scratch_shapes=[pltpu.VMEM((CONV_HALO + CONV_ROWS, D_MODEL), F32),
                        pltpu.VMEM((CONV_ROWS, D_MODEL), F32)],
        compiler_params=_params("parallel", "arbitrary"),
        name="conv_prompt",
    )(u, u, w_dw, b_dw, ln_g, ln_b)


CONV_SAMPLE_BATCH_BLOCK = 4


def _conv_sample_kernel(u_ref, w_ref, bdw_ref, g_ref, b_ref, o_ref, *, t_new):
    outs = []
    for bb in range(CONV_SAMPLE_BATCH_BLOCK):
        acc = jnp.zeros((t_new, D_MODEL), F32)
        for j in range(CONV_WIDTH):
            acc = acc + u_ref[bb, j:j + t_new, :] * w_ref[j:j + 1, :]
        outs.append(acc + bdw_ref[...])
    c = jnp.concatenate(outs, axis=0)
    o_ref[...] = _ln_swish(c, g_ref[...], b_ref[...]).astype(BF16)


def _conv_sample(u_pad, w_dw, b_dw, ln_g, ln_b, layer):
    batch, rows, _ = u_pad.shape
    t_new = rows - (CONV_WIDTH - 1)
    vec = pl.BlockSpec((None, 1, D_MODEL), lambda b: (layer, 0, 0))
    return pl.pallas_call(
        functools.partial(_conv_sample_kernel, t_new=t_new),
        grid=(batch // CONV_SAMPLE_BATCH_BLOCK,),
        in_specs=[
            pl.BlockSpec((CONV_SAMPLE_BATCH_BLOCK, rows, D_MODEL), lambda b: (b, 0, 0)),
            pl.BlockSpec((None, CONV_WIDTH, D_MODEL), lambda b: (layer, 0, 0)),
            vec, vec, vec,
        ],
        out_specs=pl.BlockSpec((CONV_SAMPLE_BATCH_BLOCK * t_new, D_MODEL), lambda b: (b, 0)),
        out_shape=jax.ShapeDtypeStruct((batch * t_new, D_MODEL), BF16),
        compiler_params=_params("parallel"),
        name="conv_sample",
    )(u_pad, w_dw, b_dw, ln_g, ln_b)


def _rope_tables(pos):
    inv = ROPE_THETA ** (-jnp.arange(0, HEAD_DIM, 2, dtype=F32) / HEAD_DIM)
    ang = pos.astype(F32)[:, None] * inv[None, :]
    cos, sin = jnp.cos(ang), jnp.sin(ang)
    reps = LANES // HEAD_DIM
    cos_t = jnp.tile(jnp.concatenate([cos, cos], axis=-1), (1, reps))
    sin_t = jnp.tile(jnp.concatenate([-sin, sin], axis=-1), (1, reps))
    scale = HEAD_DIM ** -0.5
    cos_all = jnp.concatenate([cos_t * scale, cos_t, jnp.ones_like(cos_t)], axis=1)
    sin_all = jnp.concatenate([sin_t * scale, sin_t, jnp.zeros_like(sin_t)], axis=1)
    return cos_all, sin_all


def kernel(x_prompt, x_sample, cache_k, cache_v, state_conv, norm_mix, norm_mlp, norm_final,
           attn_w_qkv, attn_b_qkv, attn_sinks, attn_w_o, attn_b_o,
           conv_w_in, conv_b_in, conv_w_dw, conv_b_dw, conv_ln_g, conv_ln_b,
           conv_w_out, conv_b_out, mlp_w_up, mlp_w_down):
    batch, seq, _ = x_prompt.shape
    dbatch, t_new, _ = x_sample.shape
    xp = x_prompt.reshape(batch * seq, D_MODEL)
    xs = x_sample.reshape(dbatch * t_new, D_MODEL)

    cos_p, sin_p = _rope_tables(jnp.arange(seq, dtype=jnp.int32))
    cos_s, sin_s = _rope_tables(PAST_LEN + jnp.arange(t_new, dtype=jnp.int32))
    cos_s = jnp.tile(cos_s, (dbatch, 1))
    sin_s = jnp.tile(sin_s, (dbatch, 1))

    norm_mix, norm_mlp = _layer_vec(norm_mix), _layer_vec(norm_mlp)
    attn_b_qkv, attn_b_o = _layer_vec(attn_b_qkv), _layer_vec(attn_b_o)
    conv_b_in, conv_b_out = _layer_vec(conv_b_in), _layer_vec(conv_b_out)
    conv_b_dw, conv_ln_g, conv_ln_b = _layer_vec(conv_b_dw), _layer_vec(conv_ln_g), _layer_vec(conv_ln_b)
    n_attn = cache_k.shape[0]
    ck = cache_k.reshape(n_attn, dbatch, WINDOW, KV_DIM)
    cv = cache_v.reshape(n_attn, dbatch, WINDOW, KV_DIM)
    sink_rows = jnp.repeat(attn_sinks, t_new, axis=1)[:, :, None]
    g_final = norm_final.reshape(1, D_MODEL)

    k_p, v_p, c_p, k_s, v_s, c_s = [], [], [], [], [], []
    for i in range(DEPTH):
        j = i // 2
        if i % 2 == 0:
            q, kv = _qkv_rope(xp, norm_mix, attn_w_qkv, attn_b_qkv, cos_p, sin_p, j, i)
            o = _attn_prompt(q, kv, attn_sinks, batch, seq, j)
            xp = _proj_res(o, attn_w_o, attn_b_o, xp, j)
            kv3 = kv.reshape(batch, seq, 2 * KV_DIM)[:, seq - WINDOW:]
            k_p.append(kv3[..., :KV_DIM].reshape(batch, WINDOW, N_KV_HEADS, HEAD_DIM))
            v_p.append(kv3[..., KV_DIM:].reshape(batch, WINDOW, N_KV_HEADS, HEAD_DIM))
            q, kv = _qkv_rope(xs, norm_mix, attn_w_qkv, attn_b_qkv, cos_s, sin_s, j, i)
            o = _attn_sample(q, kv, ck, cv, sink_rows, dbatch, t_new, j)
            xs = _proj_res(o, attn_w_o, attn_b_o, xs, j)
            kv3 = kv.reshape(dbatch, t_new, 2 * KV_DIM)
            k_new = kv3[..., :KV_DIM].reshape(dbatch, t_new, N_KV_HEADS, HEAD_DIM)
            v_new = kv3[..., KV_DIM:].reshape(dbatch, t_new, N_KV_HEADS, HEAD_DIM)
            k_s.append(jnp.concatenate([cache_k[j][:, t_new:], k_new], axis=1))
            v_s.append(jnp.concatenate([cache_v[j][:, t_new:], v_new], axis=1))
        else:
            conv_args = (conv_w_dw, conv_b_dw, conv_ln_g, conv_ln_b)
            u = _glu(xp, norm_mix, conv_w_in, conv_b_in, j, i)
            c = _conv_prompt(u, *conv_args, batch, seq, j)
            xp = _proj_res(c, conv_w_out, conv_b_out, xp, j)
            c_p.append(u.reshape(batch, seq, D_MODEL)[:, seq - (CONV_WIDTH - 1):])
            u = _glu(xs, norm_mix, conv_w_in, conv_b_in, j, i)
            u_pad = jnp.concatenate([state_conv[j], u.reshape(dbatch, t_new, D_MODEL)], axis=1)
            c = _conv_sample(u_pad, *conv_args, j)
            xs = _proj_res(c, conv_w_out, conv_b_out, xs, j)
            c_s.append(u_pad[:, t_new:])
        last = i == DEPTH - 1
        xp = _mlp(xp, norm_mlp, mlp_w_up, mlp_w_down, g_final, i, last)
        xs = _mlp(xs, norm_mlp, mlp_w_up, mlp_w_down, g_final, i, last)

    y_p = xp.reshape(batch, seq, D_MODEL)
    y_s = xs.reshape(dbatch, t_new, D_MODEL)
    return (y_p, y_s, jnp.stack(k_p), jnp.stack(v_p), jnp.stack(c_p),
            jnp.stack(k_s), jnp.stack(v_s), jnp.stack(c_s))
```

```python
import functools

import jax
import jax.numpy as jnp
from jax import lax
from jax.experimental import pallas as pl
from jax.experimental.pallas import tpu as pltpu

D_MODEL = 2048
HEAD_DIM = 64
N_HEADS = 32
N_KV_HEADS = 8
GROUP = N_HEADS // N_KV_HEADS
Q_DIM = N_HEADS * HEAD_DIM
KV_DIM = N_KV_HEADS * HEAD_DIM
WINDOW = 128
PAST_LEN = 16384
ROPE_THETA = 10000.0
CONV_WIDTH = 31
D_FF = 4 * D_MODEL
RMS_EPS = 1e-6
LN_EPS = 1e-5
DEPTH = 4

VMEM_LIMIT_BYTES = 60 * 1024 * 1024
LANES = 128
SUBLANES = 8
ROW_TILE = 1024
COL_TILE = 512
FF_TILE = 512
CONV_ROWS = 128
CONV_HALO = 32
HEAD_SLAB = GROUP * HEAD_DIM

assert COL_TILE == KV_DIM and LANES == 2 * HEAD_DIM and HEAD_SLAB == 2 * LANES

BF16 = jnp.bfloat16
F32 = jnp.float32
NT_DIMS = (((1,), (1,)), ((), ()))


def _params(*sem):
    return pltpu.CompilerParams(dimension_semantics=sem, vmem_limit_bytes=VMEM_LIMIT_BYTES)


def _rms_rows(x, g):
    ms = jnp.mean(x * x, axis=-1, keepdims=True)
    return x * lax.rsqrt(ms + RMS_EPS) * g


def _layer_vec(v):
    return v.reshape(v.shape[0], 1, v.shape[1])


def _first_pass_tile(layer, n_tiles):
    def index_map(i, j):
        return (layer, 0, jnp.where(i == 0, j, n_tiles - 1))
    return index_map


def _qkv_kernel(x_ref, g_ref, w_ref, b_ref, cos_ref, sin_ref, q_ref, kv_ref, h_ref, wbf_ref):
    i = pl.program_id(0)
    j = pl.program_id(1)
    nq = Q_DIM // COL_TILE

    @pl.when(i == 0)
    def _():
        wbf_ref[j] = w_ref[...].astype(BF16)

    @pl.when(j == 0)
    def _():
        h_ref[...] = _rms_rows(x_ref[...], g_ref[...]).astype(BF16)

    w = wbf_ref[j]
    bias = b_ref[...]
    reps = COL_TILE // LANES

    def project(rows):
        return jnp.dot(h_ref[rows, :], w, preferred_element_type=F32) + bias

    def rotate_store(rows, y):
        lane = lax.broadcasted_iota(jnp.int32, y.shape, 1)
        first_half = (lane % HEAD_DIM) < (HEAD_DIM // 2)
        rot = jnp.where(first_half,
                        pltpu.roll(y, COL_TILE - HEAD_DIM // 2, 1),
                        pltpu.roll(y, HEAD_DIM // 2, 1))
        r = (y * jnp.tile(cos_ref[rows, :], (1, reps))
             + rot * jnp.tile(sin_ref[rows, :], (1, reps)))

        @pl.when(j < nq)
        def _():
            q_ref[rows, :] = r.astype(BF16)

        @pl.when(j >= nq)
        def _():
            kv_ref[rows, :] = r

    half = h_ref.shape[0] // 2
    top, bottom = slice(0, half), slice(half, 2 * half)
    y_top = project(top)
    y_bottom = project(bottom)
    rotate_store(top, y_top)
    rotate_store(bottom, y_bottom)


def _qkv_rope(x, g, w, b, cos_t, sin_t, layer, norm_layer):
    m = x.shape[0]
    tm = min(ROW_TILE, m)
    n_tab = cos_t.shape[0] // tm
    nq = Q_DIM // COL_TILE
    nj = (Q_DIM + 2 * KV_DIM) // COL_TILE

    def tab_map(i, j):
        return (i % n_tab, jnp.maximum(j - nq + 1, 0))

    return pl.pallas_call(
        _qkv_kernel,
        grid=(m // tm, nj),
        in_specs=[
            pl.BlockSpec((tm, D_MODEL), lambda i, j: (i, 0)),
            pl.BlockSpec((None, 1, D_MODEL), lambda i, j: (norm_layer, 0, 0)),
            pl.BlockSpec((None, D_MODEL, COL_TILE), _first_pass_tile(layer, nj)),
            pl.BlockSpec((None, 1, COL_TILE), lambda i, j: (layer, 0, j)),
            pl.BlockSpec((tm, LANES), tab_map),
            pl.BlockSpec((tm, LANES), tab_map),
        ],
        out_specs=[
            pl.BlockSpec((tm, COL_TILE), lambda i, j: (i, jnp.minimum(j, nq - 1))),
            pl.BlockSpec((tm, COL_TILE), lambda i, j: (i, jnp.maximum(j - nq, 0))),
        ],
        out_shape=[
            jax.ShapeDtypeStruct((m, Q_DIM), BF16),
            jax.ShapeDtypeStruct((m, 2 * KV_DIM), F32),
        ],
        scratch_shapes=[pltpu.VMEM((tm, D_MODEL), BF16),
                        pltpu.VMEM((nj, D_MODEL, COL_TILE), BF16)],
        compiler_params=_params("arbitrary", "arbitrary"),
        name="qkv_rope",
    )(x, g, w, b, cos_t, sin_t)


def _rep_heads(x):
    n = x.shape[1]
    lane = lax.broadcasted_iota(jnp.int32, x.shape, 1)
    low = (lane % LANES) < HEAD_DIM
    even = jnp.where(low, x, pltpu.roll(x, HEAD_DIM, 1))
    odd = jnp.where(low, pltpu.roll(x, n - HEAD_DIM, 1), x)
    pieces = []
    for s in range(n // LANES):
        sl = slice(s * LANES, (s + 1) * LANES)
        pieces += [even[:, sl], even[:, sl], odd[:, sl], odd[:, sl]]
    return jnp.concatenate(pieces, axis=1)


def _attn_prompt_kernel(sinks_ref, q_ref, kc_ref, vc_ref, o_ref, kb_ref, vt_ref, ot_ref, *, layer):
    n = pl.program_id(1)

    @pl.when(n == 0)
    def _():
        kb_ref[0:WINDOW, :] = jnp.zeros((WINDOW, Q_DIM), BF16)
        vt_ref[:, 0:WINDOW] = jnp.zeros((KV_DIM, WINDOW), BF16)

    @pl.when(n > 0)
    def _():
        kb_ref[0:WINDOW, :] = kb_ref[WINDOW:, :]
        vt_ref[:, 0:WINDOW] = vt_ref[:, WINDOW:]

    kb_ref[WINDOW:, :] = _rep_heads(kc_ref[...]).astype(BF16)
    vt_ref[:, WINDOW:] = vc_ref[...].T.astype(BF16)

    cols = GROUP * WINDOW
    key = lax.broadcasted_iota(jnp.int32, (2 * WINDOW, cols), 0)
    qi = lax.broadcasted_iota(jnp.int32, (2 * WINDOW, cols), 1) % WINDOW
    allowed = jnp.logical_or(
        jnp.logical_and(key < WINDOW, jnp.logical_and(key >= qi, n > 0)),
        jnp.logical_and(key >= WINDOW, key - WINDOW <= qi))
    bias = jnp.where(allowed, 0.0, -jnp.inf).astype(F32)
    lane_group = lax.broadcasted_iota(jnp.int32, (WINDOW, HEAD_SLAB), 1) // HEAD_DIM
    col_group = lax.broadcasted_iota(jnp.int32, (1, cols), 1) // WINDOW

    def scores(h):
        hs = slice(h * HEAD_SLAB, (h + 1) * HEAD_SLAB)
        qh = q_ref[:, hs]
        zero = jnp.zeros_like(qh)
        qm = jnp.concatenate([jnp.where(lane_group == g, qh, zero) for g in range(GROUP)], axis=0)
        return lax.dot_general(kb_ref[:, hs], qm, NT_DIMS, preferred_element_type=F32) + bias

    def sink_row(h):
        sink = jnp.full((1, cols), sinks_ref[layer, h * GROUP + GROUP - 1], F32)
        for g in range(GROUP - 2, -1, -1):
            sink = jnp.where(col_group == g, sinks_ref[layer, h * GROUP + g], sink)
        return sink

    def softmax(s, sink):
        mx = jnp.maximum(jnp.max(s, axis=0, keepdims=True), sink)
        p = jnp.exp(s - mx)
        denom = jnp.sum(p, axis=0, keepdims=True) + jnp.exp(sink - mx)
        return p.astype(BF16), 1.0 / denom

    def finish(h, p, inv):
        ot = jnp.dot(vt_ref[h * HEAD_DIM:(h + 1) * HEAD_DIM, :], p, preferred_element_type=F32) * inv
        for g in range(GROUP):
            r0 = h * HEAD_SLAB + g * HEAD_DIM
            ot_ref[r0:r0 + HEAD_DIM, :] = ot[:, g * WINDOW:(g + 1) * WINDOW]

    s_next = scores(0)
    pending = None
    for h in range(N_KV_HEADS):
        s = s_next
        if h + 1 < N_KV_HEADS:
            s_next = scores(h + 1)
        p, inv = softmax(s, sink_row(h))
        if pending is not None:
            finish(*pending)
        pending = (h, p, inv)
    finish(*pending)
    o_ref[...] = ot_ref[...].T.astype(BF16)


def _attn_prompt(q, kv, sinks, batch, seq, layer):
    nb = seq // WINDOW
    return pl.pallas_call(
        functools.partial(_attn_prompt_kernel, layer=layer),
        grid=(batch, nb),
        in_specs=[
            pl.BlockSpec(memory_space=pltpu.SMEM),
            pl.BlockSpec((WINDOW, Q_DIM), lambda b, n: (b * nb + n, 0)),
            pl.BlockSpec((WINDOW, KV_DIM), lambda b, n: (b * nb + n, 0)),
            pl.BlockSpec((WINDOW, KV_DIM), lambda b, n: (b * nb + n, 1)),
        ],
        out_specs=pl.BlockSpec((WINDOW, Q_DIM), lambda b, n: (b * nb + n, 0)),
        out_shape=jax.ShapeDtypeStruct((batch * seq, Q_DIM), BF16),
        scratch_shapes=[pltpu.VMEM((2 * WINDOW, Q_DIM), BF16),
                        pltpu.VMEM((KV_DIM, 2 * WINDOW), BF16),
                        pltpu.VMEM((Q_DIM, WINDOW), F32)],
        compiler_params=_params("parallel", "arbitrary"),
        name="attn_prompt",
    )(sinks, q, kv, kv)


SAMPLE_BATCH_BLOCK = 4


def _attn_sample_kernel(sink_ref, q_ref, kvn_ref, ck_ref, cv_ref, o_ref, own_ref, *, t_new):
    pad = 2 * SUBLANES - t_new
    n_keys = WINDOW + t_new + pad
    rows = N_HEADS * t_new
    t_of_row = lax.broadcasted_iota(jnp.int32, (rows, n_keys), 0) % t_new
    col = lax.broadcasted_iota(jnp.int32, (rows, n_keys), 1)
    allowed = jnp.logical_or(
        jnp.logical_and(col < WINDOW, col >= t_of_row),
        jnp.logical_and(col >= WINDOW, col - WINDOW <= t_of_row))
    bias = jnp.where(allowed, 0.0, -jnp.inf).astype(F32)
    head_of_row = lax.broadcasted_iota(jnp.int32, (rows, Q_DIM), 0) // t_new
    head_of_lane = lax.broadcasted_iota(jnp.int32, (rows, Q_DIM), 1) // HEAD_DIM
    own_ref[...] = jnp.where(head_of_row == head_of_lane, 1.0, 0.0).astype(F32)
    sink = sink_ref[...]
    q_all = q_ref[...].astype(F32)
    kvn = kvn_ref[...]
    zpad = jnp.zeros((pad, KV_DIM), F32)

    def scores(bb):
        r0 = bb * t_new
        k_new = jnp.concatenate([kvn[r0:r0 + t_new, :KV_DIM], zpad], axis=0)
        k_all = jnp.concatenate([_rep_heads(ck_ref[bb]).astype(BF16),
                                 _rep_heads(k_new).astype(BF16)], axis=0)
        qf = (jnp.tile(q_all[r0:r0 + t_new], (N_HEADS, 1)) * own_ref[...]).astype(BF16)
        return lax.dot_general(qf, k_all, NT_DIMS, preferred_element_type=F32) + bias

    def softmax(s):
        mx = jnp.maximum(jnp.max(s, axis=-1, keepdims=True), sink)
        p = jnp.exp(s - mx)
        denom = jnp.sum(p, axis=-1, keepdims=True) + jnp.exp(sink - mx)
        return (p * (1.0 / denom)).astype(BF16)

    def finish(bb, pn):
        r0 = bb * t_new
        v_new = jnp.concatenate([kvn[r0:r0 + t_new, KV_DIM:], zpad], axis=0)
        v_all = jnp.concatenate([_rep_heads(cv_ref[bb]).astype(BF16),
                                 _rep_heads(v_new).astype(BF16)], axis=0)
        pv = jnp.dot(pn, v_all, preferred_element_type=F32) * own_ref[...]
        o = pv[0:t_new]
        for h in range(1, N_HEADS):
            o = o + pv[h * t_new:(h + 1) * t_new]
        o_ref[r0:r0 + t_new, :] = o.astype(BF16)

    s_next = scores(0)
    pending = None
    for bb in range(SAMPLE_BATCH_BLOCK):
        s = s_next
        if bb + 1 < SAMPLE_BATCH_BLOCK:
            s_next = scores(bb + 1)
        pn = softmax(s)
        if pending is not None:
            finish(*pending)
        pending = (bb, pn)
    finish(*pending)


def _attn_sample(q, kv, cache_k, cache_v, sink_rows, batch, t_new, layer):
    rows = SAMPLE_BATCH_BLOCK * t_new
    cache_spec = pl.BlockSpec((None, SAMPLE_BATCH_BLOCK, WINDOW, KV_DIM), lambda b: (layer, b, 0, 0))
    return pl.pallas_call(
        functools.partial(_attn_sample_kernel, t_new=t_new),
        grid=(batch // SAMPLE_BATCH_BLOCK,),
        in_specs=[
            pl.BlockSpec((None, N_HEADS * t_new, 1), lambda b: (layer, 0, 0)),
            pl.BlockSpec((rows, Q_DIM), lambda b: (b, 0)),
            pl.BlockSpec((rows, 2 * KV_DIM), lambda b: (b, 0)),
            cache_spec,
            cache_spec,
        ],
        out_specs=pl.BlockSpec((rows, Q_DIM), lambda b: (b, 0)),
        out_shape=jax.ShapeDtypeStruct((batch * t_new, Q_DIM), BF16),
        scratch_shapes=[pltpu.VMEM((N_HEADS * t_new, Q_DIM), F32)],
        compiler_params=_params("parallel"),
        name="attn_sample",
    )(sink_rows, q, kv, cache_k, cache_v)


def _proj_res_kernel(a_ref, w_ref, b_ref, res_ref, o_ref, wbf_ref):
    j = pl.program_id(1)

    @pl.when(pl.program_id(0) == 0)
    def _():
        wbf_ref[j] = w_ref[...].astype(BF16)

    y = jnp.dot(a_ref[...], wbf_ref[j], preferred_element_type=F32)
    o_ref[...] = res_ref[...] + y + b_ref[...]


def _proj_res(a, w, b, res, layer):
    m, k = a.shape
    n = w.shape[-1]
    tm = min(ROW_TILE, m)
    nj = n // COL_TILE
    return pl.pallas_call(
        _proj_res_kernel,
        grid=(m // tm, nj),
        in_specs=[
            pl.BlockSpec((tm, k), lambda i, j: (i, 0)),
            pl.BlockSpec((None, k, COL_TILE), _first_pass_tile(layer, nj)),
            pl.BlockSpec((None, 1, COL_TILE), lambda i, j: (layer, 0, j)),
            pl.BlockSpec((tm, COL_TILE), lambda i, j: (i, j)),
        ],
        out_specs=pl.BlockSpec((tm, COL_TILE), lambda i, j: (i, j)),
        out_shape=jax.ShapeDtypeStruct((m, n), F32),
        scratch_shapes=[pltpu.VMEM((nj, k, COL_TILE), BF16)],
        compiler_params=_params("arbitrary", "arbitrary"),
        name="proj_res",
    )(a, w, b, res)


def _mlp_kernel(x_ref, g_ref, wu_ref, wd_ref, gf_ref, o_ref, h_ref, *, final_norm):
    f = pl.program_id(1)

    @pl.when(f == 0)
    def _():
        x = x_ref[...]
        h_ref[...] = _rms_rows(x, g_ref[...]).astype(BF16)
        o_ref[...] = x

    a = jnp.dot(h_ref[...], wu_ref[...].astype(BF16), preferred_element_type=F32)
    a = jnp.maximum(a, 0.0)
    a = (a * a).astype(BF16)
    o_ref[...] += jnp.dot(a, wd_ref[...].astype(BF16), preferred_element_type=F32)

    if final_norm:
        @pl.when(f == pl.num_programs(1) - 1)
        def _():
            o_ref[...] = _rms_rows(o_ref[...], gf_ref[...])


def _mlp(x, g, w_up, w_down, g_final, layer, final_norm):
    m = x.shape[0]
    tm = min(ROW_TILE, m)
    return pl.pallas_call(
        functools.partial(_mlp_kernel, final_norm=final_norm),
        grid=(m // tm, D_FF // FF_TILE),
        in_specs=[
            pl.BlockSpec((tm, D_MODEL), lambda i, f: (i, 0)),
            pl.BlockSpec((None, 1, D_MODEL), lambda i, f: (layer, 0, 0)),
            pl.BlockSpec((None, D_MODEL, FF_TILE), lambda i, f: (layer, 0, f)),
            pl.BlockSpec((None, FF_TILE, D_MODEL), lambda i, f: (layer, f, 0)),
            pl.BlockSpec((1, D_MODEL), lambda i, f: (0, 0)),
        ],
        out_specs=pl.BlockSpec((tm, D_MODEL), lambda i, f: (i, 0)),
        out_shape=jax.ShapeDtypeStruct((m, D_MODEL), F32),
        scratch_shapes=[pltpu.VMEM((tm, D_MODEL), BF16)],
        compiler_params=_params("parallel", "arbitrary"),
        name="mlp",
    )(x, g, w_up, w_down, g_final)


def _glu_kernel(x_ref, g_ref, wa_ref, wg_ref, ba_ref, bg_ref, u_ref, h_ref, wbf_ref):
    j = pl.program_id(1)
    nj = D_MODEL // COL_TILE

    @pl.when(pl.program_id(0) == 0)
    def _():
        wbf_ref[j] = wa_ref[...].astype(BF16)
        wbf_ref[nj + j] = wg_ref[...].astype(BF16)

    @pl.when(j == 0)
    def _():
        h_ref[...] = _rms_rows(x_ref[...], g_ref[...]).astype(BF16)

    wa, wg = wbf_ref[j], wbf_ref[nj + j]

    def project(rows):
        h = h_ref[rows, :]
        return (jnp.dot(h, wa, preferred_element_type=F32) + ba_ref[...],
                jnp.dot(h, wg, preferred_element_type=F32) + bg_ref[...])

    half = h_ref.shape[0] // 2
    top, bottom = slice(0, half), slice(half, 2 * half)
    a_top, gate_top = project(top)
    a_bottom, gate_bottom = project(bottom)
    u_ref[top, :] = a_top * jax.nn.sigmoid(gate_top)
    u_ref[bottom, :] = a_bottom * jax.nn.sigmoid(gate_bottom)


def _glu(x, g, w_in, b_in, layer, norm_layer):
    m = x.shape[0]
    tm = min(ROW_TILE, m)
    nj = D_MODEL // COL_TILE

    def gate_tile(i, j):
        return (layer, 0, nj + jnp.where(i == 0, j, nj - 1))

    return pl.pallas_call(
        _glu_kernel,
        grid=(m // tm, nj),
        in_specs=[
            pl.BlockSpec((tm, D_MODEL), lambda i, j: (i, 0), pipeline_mode=pl.Buffered(1)),
            pl.BlockSpec((None, 1, D_MODEL), lambda i, j: (norm_layer, 0, 0)),
            pl.BlockSpec((None, D_MODEL, COL_TILE), _first_pass_tile(layer, nj)),
            pl.BlockSpec((None, D_MODEL, COL_TILE), gate_tile),
            pl.BlockSpec((None, 1, COL_TILE), lambda i, j: (layer, 0, j)),
            pl.BlockSpec((None, 1, COL_TILE), lambda i, j: (layer, 0, j + nj)),
        ],
        out_specs=pl.BlockSpec((tm, COL_TILE), lambda i, j: (i, j)),
        out_shape=jax.ShapeDtypeStruct((m, D_MODEL), F32),
        scratch_shapes=[pltpu.VMEM((tm, D_MODEL), BF16),
                        pltpu.VMEM((2 * nj, D_MODEL, COL_TILE), BF16)],
        compiler_params=_params("arbitrary", "arbitrary"),
        name="glu",
    )(x, g, w_in, w_in, b_in, b_in)


def _ln_swish(c, g, b):
    mu = jnp.mean(c, axis=-1, keepdims=True)
    xc = c - mu
    var = jnp.mean(xc * xc, axis=-1, keepdims=True)
    y = xc * lax.rsqrt(var + LN_EPS) * g + b
    return y * jax.nn.sigmoid(y)


def _conv_prompt_kernel(cur_ref, halo_ref, w_ref, bdw_ref, g_ref, b_ref, o_ref, win_ref, c_ref):
    t = pl.program_id(1)
    halo = halo_ref[...]
    win_ref[0:CONV_HALO, :] = jnp.where(t > 0, halo, jnp.zeros_like(halo))
    win_ref[CONV_HALO:, :] = cur_ref[...]
    off = CONV_HALO - (CONV_WIDTH - 1)
    n_win = CONV_HALO + CONV_ROWS
    for c in range(D_MODEL // LANES):
        cs = slice(c * LANES, (c + 1) * LANES)
        win = win_ref[:, cs]
        acc = jnp.zeros((CONV_ROWS, LANES), F32)
        for phase in range(SUBLANES):
            taps = [j for j in range(CONV_WIDTH) if (off + j) % SUBLANES == phase]
            if not taps:
                continue
            shifted = win if phase == 0 else pltpu.roll(win, n_win - phase, 0)
            for j in taps:
                base = (off + j) - phase
                acc = acc + shifted[base:base + CONV_ROWS] * w_ref[j:j + 1, cs]
        c_ref[:, cs] = acc + bdw_ref[:, cs]
    o_ref[...] = _ln_swish(c_ref[...], g_ref[...], b_ref[...]).astype(BF16)


def _conv_prompt(u, w_dw, b_dw, ln_g, ln_b, batch, seq, layer):
    nt = seq // CONV_ROWS
    ratio = CONV_ROWS // CONV_HALO
    vec = pl.BlockSpec((None, 1, D_MODEL), lambda b, t: (layer, 0, 0))
    return pl.pallas_call(
        _conv_prompt_kernel,
        grid=(batch, nt),
        in_specs=[
            pl.BlockSpec((CONV_ROWS, D_MODEL), lambda b, t: (b * nt + t, 0)),
            pl.BlockSpec((CONV_HALO, D_MODEL),
                         lambda b, t: (jnp.maximum((b * nt + t) * ratio - 1, 0), 0)),
            pl.BlockSpec((None, CONV_WIDTH, D_MODEL), lambda b, t: (layer, 0, 0)),
            vec, vec, vec,
        ],
        out_specs=pl.BlockSpec((CONV_ROWS, D_MODEL), lambda b, t: (b * nt + t, 0)),
        out_shape=jax.ShapeDtypeStruct((batch * seq, D_MODEL), BF16),
        scratch_shapes=[pltpu.VMEM((CONV_HALO + CONV_ROWS, D_MODEL), F32),
                        pltpu.VMEM((CONV_ROWS, D_MODEL), F32)],
        compiler_params=_params("parallel", "arbitrary"),
        name="conv_prompt",
    )(u, u, w_dw, b_dw, ln_g, ln_b)


CONV_SAMPLE_BATCH_BLOCK = 4


def _conv_sample_kernel(u_ref, w_ref, bdw_ref, g_ref, b_ref, o_ref, *, t_new):
    outs = []
    for bb in range(CONV_SAMPLE_BATCH_BLOCK):
        acc = jnp.zeros((t_new, D_MODEL), F32)
        for j in range(CONV_WIDTH):
            acc = acc + u_ref[bb, j:j + t_new, :] * w_ref[j:j + 1, :]
        outs.append(acc + bdw_ref[...])
    c = jnp.concatenate(outs, axis=0)
    o_ref[...] = _ln_swish(c, g_ref[...], b_ref[...]).astype(BF16)


def _conv_sample(u_pad, w_dw, b_dw, ln_g, ln_b, layer):
    batch, rows, _ = u_pad.shape
    t_new = rows - (CONV_WIDTH - 1)
    vec = pl.BlockSpec((None, 1, D_MODEL), lambda b: (layer, 0, 0))
    return pl.pallas_call(
        functools.partial(_conv_sample_kernel, t_new=t_new),
        grid=(batch // CONV_SAMPLE_BATCH_BLOCK,),
        in_specs=[
            pl.BlockSpec((CONV_SAMPLE_BATCH_BLOCK, rows, D_MODEL), lambda b: (b, 0, 0)),
            pl.BlockSpec((None, CONV_WIDTH, D_MODEL), lambda b: (layer, 0, 0)),
            vec, vec, vec,
        ],
        out_specs=pl.BlockSpec((CONV_SAMPLE_BATCH_BLOCK * t_new, D_MODEL), lambda b: (b, 0)),
        out_shape=jax.ShapeDtypeStruct((batch * t_new, D_MODEL), BF16),
        compiler_params=_params("parallel"),
        name="conv_sample",
    )(u_pad, w_dw, b_dw, ln_g, ln_b)


def _rope_tables(pos):
    inv = ROPE_THETA ** (-jnp.arange(0, HEAD_DIM, 2, dtype=F32) / HEAD_DIM)
    ang = pos.astype(F32)[:, None] * inv[None, :]
    cos, sin = jnp.cos(ang), jnp.sin(ang)
    reps = LANES // HEAD_DIM
    cos_t = jnp.tile(jnp.concatenate([cos, cos], axis=-1), (1, reps))
    sin_t = jnp.tile(jnp.concatenate([-sin, sin], axis=-1), (1, reps))
    scale = HEAD_DIM ** -0.5
    cos_all = jnp.concatenate([cos_t * scale, cos_t, jnp.ones_like(cos_t)], axis=1)
    sin_all = jnp.concatenate([sin_t * scale, sin_t, jnp.zeros_like(sin_t)], axis=1)
    return cos_all, sin_all


def kernel(x_prompt, x_sample, cache_k, cache_v, state_conv, norm_mix, norm_mlp, norm_final,
           attn_w_qkv, attn_b_qkv, attn_sinks, attn_w_o, attn_b_o,
           conv_w_in, conv_b_in, conv_w_dw, conv_b_dw, conv_ln_g, conv_ln_b,
           conv_w_out, conv_b_out, mlp_w_up, mlp_w_down):
    batch, seq, _ = x_prompt.shape
    dbatch, t_new, _ = x_sample.shape
    xp = x_prompt.reshape(batch * seq, D_MODEL)
    xs = x_sample.reshape(dbatch * t_new, D_MODEL)

    cos_p, sin_p = _rope_tables(jnp.arange(seq, dtype=jnp.int32))
    cos_s, sin_s = _rope_tables(PAST_LEN + jnp.arange(t_new, dtype=jnp.int32))
    cos_s = jnp.tile(cos_s, (dbatch, 1))
    sin_s = jnp.tile(sin_s, (dbatch, 1))

    norm_mix, norm_mlp = _layer_vec(norm_mix), _layer_vec(norm_mlp)
    attn_b_qkv, attn_b_o = _layer_vec(attn_b_qkv), _layer_vec(attn_b_o)
    conv_b_in, conv_b_out = _layer_vec(conv_b_in), _layer_vec(conv_b_out)
    conv_b_dw, conv_ln_g, conv_ln_b = _layer_vec(conv_b_dw), _layer_vec(conv_ln_g), _layer_vec(conv_ln_b)
    n_attn = cache_k.shape[0]
    ck = cache_k.reshape(n_attn, dbatch, WINDOW, KV_DIM)
    cv = cache_v.reshape(n_attn, dbatch, WINDOW, KV_DIM)
    sink_rows = jnp.repeat(attn_sinks, t_new, axis=1)[:, :, None]
    g_final = norm_final.reshape(1, D_MODEL)

    k_p, v_p, c_p, k_s, v_s, c_s = [], [], [], [], [], []
    for i in range(DEPTH):
        j = i // 2
        if i % 2 == 0:
            q, kv = _qkv_rope(xp, norm_mix, attn_w_qkv, attn_b_qkv, cos_p, sin_p, j, i)
            o = _attn_prompt(q, kv, attn_sinks, batch, seq, j)
            xp = _proj_res(o, attn_w_o, attn_b_o, xp, j)
            kv3 = kv.reshape(batch, seq, 2 * KV_DIM)[:, seq - WINDOW:]
            k_p.append(kv3[..., :KV_DIM].reshape(batch, WINDOW, N_KV_HEADS, HEAD_DIM))
            v_p.append(kv3[..., KV_DIM:].reshape(batch, WINDOW, N_KV_HEADS, HEAD_DIM))
            q, kv = _qkv_rope(xs, norm_mix, attn_w_qkv, attn_b_qkv, cos_s, sin_s, j, i)
            o = _attn_sample(q, kv, ck, cv, sink_rows, dbatch, t_new, j)
            xs = _proj_res(o, attn_w_o, attn_b_o, xs, j)
            kv3 = kv.reshape(dbatch, t_new, 2 * KV_DIM)
            k_new = kv3[..., :KV_DIM].reshape(dbatch, t_new, N_KV_HEADS, HEAD_DIM)
            v_new = kv3[..., KV_DIM:].reshape(dbatch, t_new, N_KV_HEADS, HEAD_DIM)
            k_s.append(jnp.concatenate([cache_k[j][:, t_new:], k_new], axis=1))
            v_s.append(jnp.concatenate([cache_v[j][:, t_new:], v_new], axis=1))
        else:
            conv_args = (conv_w_dw, conv_b_dw, conv_ln_g, conv_ln_b)
            u = _glu(xp, norm_mix, conv_w_in, conv_b_in, j, i)
            c = _conv_prompt(u, *conv_args, batch, seq, j)
            xp = _proj_res(c, conv_w_out, conv_b_out, xp, j)
            c_p.append(u.reshape(batch, seq, D_MODEL)[:, seq - (CONV_WIDTH - 1):])
            u = _glu(xs, norm_mix, conv_w_in, conv_b_in, j, i)
            u_pad = jnp.concatenate([state_conv[j], u.reshape(dbatch, t_new, D_MODEL)], axis=1)
            c = _conv_sample(u_pad, *conv_args, j)
            xs = _proj_res(c, conv_w_out, conv_b_out, xs, j)
            c_s.append(u_pad[:, t_new:])
        last = i == DEPTH - 1
        xp = _mlp(xp, norm_mlp, mlp_w_up, mlp_w_down, g_final, i, last)
        xs = _mlp(xs, norm_mlp, mlp_w_up, mlp_w_down, g_final, i, last)

    y_p = xp.reshape(batch, seq, D_MODEL)
    y_s = xs.reshape(dbatch, t_new, D_MODEL)
    return (y_p, y_s, jnp.stack(k_p), jnp.stack(v_p), jnp.stack(c_p),
            jnp.stack(k_s), jnp.stack(v_s), jnp.stack(c_s))
```

```python
import functools

import jax
import jax.numpy as jnp
from jax import lax
from jax.experimental import pallas as pl
from jax.experimental.pallas import tpu as pltpu

D_MODEL = 2048
HEAD_DIM = 64
N_HEADS = 32
N_KV_HEADS = 8
GROUP = N_HEADS // N_KV_HEADS
Q_DIM = N_HEADS * HEAD_DIM
KV_DIM = N_KV_HEADS * HEAD_DIM
WINDOW = 128
PAST_LEN = 16384
ROPE_THETA = 10000.0
CONV_WIDTH = 31
D_FF = 4 * D_MODEL
RMS_EPS = 1e-6
LN_EPS = 1e-5
DEPTH = 4

VMEM_LIMIT_BYTES = 60 * 1024 * 1024
MLP_VMEM_LIMIT_BYTES = 62 * 1024 * 1024
LANES = 128
SUBLANES = 8
ROW_TILE = 1024
COL_TILE = 512
FF_TILE = 512
CONV_ROWS = 128
CONV_HALO = 32
HEAD_SLAB = GROUP * HEAD_DIM

assert COL_TILE == KV_DIM and LANES == 2 * HEAD_DIM and HEAD_SLAB == 2 * LANES

BF16 = jnp.bfloat16
F32 = jnp.float32
NT_DIMS = (((1,), (1,)), ((), ()))


def _params(*sem):
    return pltpu.CompilerParams(dimension_semantics=sem, vmem_limit_bytes=VMEM_LIMIT_BYTES)


def _rms_rows(x, g):
    ms = jnp.mean(x * x, axis=-1, keepdims=True)
    return x * lax.rsqrt(ms + RMS_EPS) * g


def _layer_vec(v):
    return v.reshape(v.shape[0], 1, v.shape[1])


def _first_pass_tile(layer, n_tiles):
    def index_map(i, j):
        return (layer, 0, jnp.where(i == 0, j, n_tiles - 1))
    return index_map


def _last_tile_only(n_row_tiles, col_map):
    def index_map(i, j):
        return (0, jnp.where(i == n_row_tiles - 1, col_map(j), col_map(0)))
    return index_map


def _qkv_kernel(x_ref, xs_ref, g_ref, w_ref, b_ref, cos_ref, sin_ref, coss_ref, sins_ref,
                q_ref, kv_ref, qs_ref, kvs_ref, h_ref, hs_ref, wbf_ref):
    i = pl.program_id(0)
    j = pl.program_id(1)
    nq = Q_DIM // COL_TILE

    @pl.when(i == 0)
    def _():
        wbf_ref[j] = w_ref[...].astype(BF16)

    w = wbf_ref[j]
    bias = b_ref[...]
    reps = COL_TILE // LANES

    def normalize(x_r, h_r):
        h_r[...] = _rms_rows(x_r[...], g_ref[...]).astype(BF16)

    def project(h_r, rows):
        return jnp.dot(h_r[rows, :], w, preferred_element_type=F32) + bias

    def rotate_store(rows, y, cos_r, sin_r, q_r, kv_r):
        lane = lax.broadcasted_iota(jnp.int32, y.shape, 1)
        first_half = (lane % HEAD_DIM) < (HEAD_DIM // 2)
        rot = jnp.where(first_half,
                        pltpu.roll(y, COL_TILE - HEAD_DIM // 2, 1),
                        pltpu.roll(y, HEAD_DIM // 2, 1))
        r = (y * jnp.tile(cos_r[rows, :], (1, reps))
             + rot * jnp.tile(sin_r[rows, :], (1, reps)))

        @pl.when(j < nq)
        def _():
            q_r[rows, :] = r.astype(BF16)

        @pl.when(j >= nq)
        def _():
            kv_r[rows, :] = r

    @pl.when(j == 0)
    def _():
        normalize(x_ref, h_ref)

    half = h_ref.shape[0] // 2
    top, bottom = slice(0, half), slice(half, 2 * half)
    y_top = project(h_ref, top)
    y_bottom = project(h_ref, bottom)
    rotate_store(top, y_top, cos_ref, sin_ref, q_ref, kv_ref)
    rotate_store(bottom, y_bottom, cos_ref, sin_ref, q_ref, kv_ref)

    @pl.when(i == pl.num_programs(0) - 1)
    def _():
        @pl.when(j == 0)
        def _():
            normalize(xs_ref, hs_ref)

        rows = slice(0, hs_ref.shape[0])
        rotate_store(rows, project(hs_ref, rows), coss_ref, sins_ref, qs_ref, kvs_ref)


def _qkv_rope(x, xs, g, w, b, tabs, tabs_s, layer, norm_layer):
    m, ms = x.shape[0], xs.shape[0]
    tm = min(ROW_TILE, m)
    ni = m // tm
    cos_t, sin_t = tabs
    cos_s, sin_s = tabs_s
    n_tab = cos_t.shape[0] // tm
    nq = Q_DIM // COL_TILE
    nj = (Q_DIM + 2 * KV_DIM) // COL_TILE

    def tab_col(j):
        return jnp.maximum(j - nq + 1, 0)

    def q_col(j):
        return jnp.minimum(j, nq - 1)

    def kv_col(j):
        return jnp.maximum(j - nq, 0)

    def tab_map(i, j):
        return (i % n_tab, tab_col(j))

    return pl.pallas_call(
        _qkv_kernel,
        grid=(ni, nj),
        in_specs=[
            pl.BlockSpec((tm, D_MODEL), lambda i, j: (i, 0)),
            pl.BlockSpec((ms, D_MODEL), lambda i, j: (0, 0)),
            pl.BlockSpec((None, 1, D_MODEL), lambda i, j: (norm_layer, 0, 0)),
            pl.BlockSpec((None, D_MODEL, COL_TILE), _first_pass_tile(layer, nj)),
            pl.BlockSpec((None, 1, COL_TILE), lambda i, j: (layer, 0, j)),
            pl.BlockSpec((tm, LANES), tab_map),
            pl.BlockSpec((tm, LANES), tab_map),
            pl.BlockSpec((ms, LANES), _last_tile_only(ni, tab_col)),
            pl.BlockSpec((ms, LANES), _last_tile_only(ni, tab_col)),
        ],
        out_specs=[
            pl.BlockSpec((tm, COL_TILE), lambda i, j: (i, q_col(j))),
            pl.BlockSpec((tm, COL_TILE), lambda i, j: (i, kv_col(j))),
            pl.BlockSpec((ms, COL_TILE), _last_tile_only(ni, q_col)),
            pl.BlockSpec((ms, COL_TILE), _last_tile_only(ni, kv_col)),
        ],
        out_shape=[
            jax.ShapeDtypeStruct((m, Q_DIM), BF16),
            jax.ShapeDtypeStruct((m, 2 * KV_DIM), F32),
            jax.ShapeDtypeStruct((ms, Q_DIM), BF16),
            jax.ShapeDtypeStruct((ms, 2 * KV_DIM), F32),
        ],
        scratch_shapes=[pltpu.VMEM((tm, D_MODEL), BF16),
                        pltpu.VMEM((ms, D_MODEL), BF16),
                        pltpu.VMEM((nj, D_MODEL, COL_TILE), BF16)],
        compiler_params=_params("arbitrary", "arbitrary"),
        name="qkv_rope",
    )(x, xs, g, w, b, cos_t, sin_t, cos_s, sin_s)


def _rep_heads(x):
    n = x.shape[1]
    lane = lax.broadcasted_iota(jnp.int32, x.shape, 1)
    low = (lane % LANES) < HEAD_DIM
    even = jnp.where(low, x, pltpu.roll(x, HEAD_DIM, 1))
    odd = jnp.where(low, pltpu.roll(x, n - HEAD_DIM, 1), x)
    pieces = []
    for s in range(n // LANES):
        sl = slice(s * LANES, (s + 1) * LANES)
        pieces += [even[:, sl], even[:, sl], odd[:, sl], odd[:, sl]]
    return jnp.concatenate(pieces, axis=1)


def _attn_prompt_kernel(sinks_ref, q_ref, kc_ref, vc_ref, o_ref, kb_ref, vt_ref, ot_ref, *, layer):
    n = pl.program_id(1)

    @pl.when(n == 0)
    def _():
        kb_ref[0:WINDOW, :] = jnp.zeros((WINDOW, Q_DIM), BF16)
        vt_ref[:, 0:WINDOW] = jnp.zeros((KV_DIM, WINDOW), BF16)

    @pl.when(n > 0)
    def _():
        kb_ref[0:WINDOW, :] = kb_ref[WINDOW:, :]
        vt_ref[:, 0:WINDOW] = vt_ref[:, WINDOW:]

    kb_ref[WINDOW:, :] = _rep_heads(kc_ref[...]).astype(BF16)
    vt_ref[:, WINDOW:] = vc_ref[...].T.astype(BF16)

    cols = GROUP * WINDOW
    key = lax.broadcasted_iota(jnp.int32, (2 * WINDOW, cols), 0)
    qi = lax.broadcasted_iota(jnp.int32, (2 * WINDOW, cols), 1) % WINDOW
    allowed = jnp.logical_or(
        jnp.logical_and(key < WINDOW, jnp.logical_and(key >= qi, n > 0)),
        jnp.logical_and(key >= WINDOW, key - WINDOW <= qi))
    bias = jnp.where(allowed, 0.0, -jnp.inf).astype(F32)
    lane_group = lax.broadcasted_iota(jnp.int32, (WINDOW, HEAD_SLAB), 1) // HEAD_DIM
    col_group = lax.broadcasted_iota(jnp.int32, (1, cols), 1) // WINDOW

    def scores(h):
        hs = slice(h * HEAD_SLAB, (h + 1) * HEAD_SLAB)
        qh = q_ref[:, hs]
        zero = jnp.zeros_like(qh)
        qm = jnp.concatenate([jnp.where(lane_group == g, qh, zero) for g in range(GROUP)], axis=0)
        return lax.dot_general(kb_ref[:, hs], qm, NT_DIMS, preferred_element_type=F32) + bias

    def sink_row(h):
        sink = jnp.full((1, cols), sinks_ref[layer, h * GROUP + GROUP - 1], F32)
        for g in range(GROUP - 2, -1, -1):
            sink = jnp.where(col_group == g, sinks_ref[layer, h * GROUP + g], sink)
        return sink

    def softmax(s, sink):
        mx = jnp.maximum(jnp.max(s, axis=0, keepdims=True), sink)
        p = jnp.exp(s - mx)
        denom = jnp.sum(p, axis=0, keepdims=True) + jnp.exp(sink - mx)
        return p.astype(BF16), 1.0 / denom

    def finish(h, p, inv):
        ot = jnp.dot(vt_ref[h * HEAD_DIM:(h + 1) * HEAD_DIM, :], p, preferred_element_type=F32) * inv
        for g in range(GROUP):
            r0 = h * HEAD_SLAB + g * HEAD_DIM
            ot_ref[r0:r0 + HEAD_DIM, :] = ot[:, g * WINDOW:(g + 1) * WINDOW]

    s_next = scores(0)
    pending = None
    for h in range(N_KV_HEADS):
        s = s_next
        if h + 1 < N_KV_HEADS:
            s_next = scores(h + 1)
        p, inv = softmax(s, sink_row(h))
        if pending is not None:
            finish(*pending)
        pending = (h, p, inv)
    finish(*pending)
    o_ref[...] = ot_ref[...].T.astype(BF16)


def _attn_prompt(q, kv, sinks, batch, seq, layer):
    nb = seq // WINDOW
    return pl.pallas_call(
        functools.partial(_attn_prompt_kernel, layer=layer),
        grid=(batch, nb),
        in_specs=[
            pl.BlockSpec(memory_space=pltpu.SMEM),
            pl.BlockSpec((WINDOW, Q_DIM), lambda b, n: (b * nb + n, 0)),
            pl.BlockSpec((WINDOW, KV_DIM), lambda b, n: (b * nb + n, 0)),
            pl.BlockSpec((WINDOW, KV_DIM), lambda b, n: (b * nb + n, 1)),
        ],
        out_specs=pl.BlockSpec((WINDOW, Q_DIM), lambda b, n: (b * nb + n, 0)),
        out_shape=jax.ShapeDtypeStruct((batch * seq, Q_DIM), BF16),
        scratch_shapes=[pltpu.VMEM((2 * WINDOW, Q_DIM), BF16),
                        pltpu.VMEM((KV_DIM, 2 * WINDOW), BF16),
                        pltpu.VMEM((Q_DIM, WINDOW), F32)],
        compiler_params=_params("parallel", "arbitrary"),
        name="attn_prompt",
    )(sinks, q, kv, kv)


SAMPLE_BATCH_BLOCK = 4


def _attn_sample_kernel(sink_ref, q_ref, kvn_ref, ck_ref, cv_ref, o_ref, own_ref, *, t_new):
    pad = 2 * SUBLANES - t_new
    n_keys = WINDOW + t_new + pad
    rows = N_HEADS * t_new
    t_of_row = lax.broadcasted_iota(jnp.int32, (rows, n_keys), 0) % t_new
    col = lax.broadcasted_iota(jnp.int32, (rows, n_keys), 1)
    allowed = jnp.logical_or(
        jnp.logical_and(col < WINDOW, col >= t_of_row),
        jnp.logical_and(col >= WINDOW, col - WINDOW <= t_of_row))
    bias = jnp.where(allowed, 0.0, -jnp.inf).astype(F32)
    head_of_row = lax.broadcasted_iota(jnp.int32, (rows, Q_DIM), 0) // t_new
    head_of_lane = lax.broadcasted_iota(jnp.int32, (rows, Q_DIM), 1) // HEAD_DIM
    own_ref[...] = jnp.where(head_of_row == head_of_lane, 1.0, 0.0).astype(F32)
    sink = sink_ref[...]
    q_all = q_ref[...].astype(F32)
    kvn = kvn_ref[...]
    zpad = jnp.zeros((pad, KV_DIM), F32)

    def scores(bb):
        r0 = bb * t_new
        k_new = jnp.concatenate([kvn[r0:r0 + t_new, :KV_DIM], zpad], axis=0)
        k_all = jnp.concatenate([_rep_heads(ck_ref[bb]).astype(BF16),
                                 _rep_heads(k_new).astype(BF16)], axis=0)
        qf = (jnp.tile(q_all[r0:r0 + t_new], (N_HEADS, 1)) * own_ref[...]).astype(BF16)
        return lax.dot_general(qf, k_all, NT_DIMS, preferred_element_type=F32) + bias

    def softmax(s):
        mx = jnp.maximum(jnp.max(s, axis=-1, keepdims=True), sink)
        p = jnp.exp(s - mx)
        denom = jnp.sum(p, axis=-1, keepdims=True) + jnp.exp(sink - mx)
        return (p * (1.0 / denom)).astype(BF16)

    def finish(bb, pn):
        r0 = bb * t_new
        v_new = jnp.concatenate([kvn[r0:r0 + t_new, KV_DIM:], zpad], axis=0)
        v_all = jnp.concatenate([_rep_heads(cv_ref[bb]).astype(BF16),
                                 _rep_heads(v_new).astype(BF16)], axis=0)
        pv = jnp.dot(pn, v_all, preferred_element_type=F32) * own_ref[...]
        o = pv[0:t_new]
        for h in range(1, N_HEADS):
            o = o + pv[h * t_new:(h + 1) * t_new]
        o_ref[r0:r0 + t_new, :] = o.astype(BF16)

    s_next = scores(0)
    pending = None
    for bb in range(SAMPLE_BATCH_BLOCK):
        s = s_next
        if bb + 1 < SAMPLE_BATCH_BLOCK:
            s_next = scores(bb + 1)
        pn = softmax(s)
        if pending is not None:
            finish(*pending)
        pending = (bb, pn)
    finish(*pending)


def _attn_sample(q, kv, cache_k, cache_v, sink_rows, batch, t_new, layer):
    rows = SAMPLE_BATCH_BLOCK * t_new
    cache_spec = pl.BlockSpec((None, SAMPLE_BATCH_BLOCK, WINDOW, KV_DIM), lambda b: (layer, b, 0, 0))
    return pl.pallas_call(
        functools.partial(_attn_sample_kernel, t_new=t_new),
        grid=(batch // SAMPLE_BATCH_BLOCK,),
        in_specs=[
            pl.BlockSpec((None, N_HEADS * t_new, 1), lambda b: (layer, 0, 0)),
            pl.BlockSpec((rows, Q_DIM), lambda b: (b, 0)),
            pl.BlockSpec((rows, 2 * KV_DIM), lambda b: (b, 0)),
            cache_spec,
            cache_spec,
        ],
        out_specs=pl.BlockSpec((rows, Q_DIM), lambda b: (b, 0)),
        out_shape=jax.ShapeDtypeStruct((batch * t_new, Q_DIM), BF16),
        scratch_shapes=[pltpu.VMEM((N_HEADS * t_new, Q_DIM), F32)],
        compiler_params=_params("parallel"),
        name="attn_sample",
    )(sink_rows, q, kv, cache_k, cache_v)


def _proj_res_kernel(a_ref, as_ref, w_ref, b_ref, res_ref, ress_ref, o_ref, os_ref, wbf_ref):
    i = pl.program_id(0)
    j = pl.program_id(1)

    @pl.when(i == 0)
    def _():
        wbf_ref[j] = w_ref[...].astype(BF16)

    def project(a_r, res_r, o_r):
        y = jnp.dot(a_r[...], wbf_ref[j], preferred_element_type=F32)
        o_r[...] = res_r[...] + y + b_ref[...]

    project(a_ref, res_ref, o_ref)

    @pl.when(i == pl.num_programs(0) - 1)
    def _():
        project(as_ref, ress_ref, os_ref)


def _proj_res(a, a_s, w, b, res, res_s, layer):
    m, k = a.shape
    ms = a_s.shape[0]
    n = w.shape[-1]
    tm = min(ROW_TILE, m)
    ni = m // tm
    nj = n // COL_TILE
    sample_tile = _last_tile_only(ni, lambda j: j)
    return pl.pallas_call(
        _proj_res_kernel,
        grid=(ni, nj),
        in_specs=[
            pl.BlockSpec((tm, k), lambda i, j: (i, 0)),
            pl.BlockSpec((ms, k), lambda i, j: (0, 0)),
            pl.BlockSpec((None, k, COL_TILE), _first_pass_tile(layer, nj)),
            pl.BlockSpec((None, 1, COL_TILE), lambda i, j: (layer, 0, j)),
            pl.BlockSpec((tm, COL_TILE), lambda i, j: (i, j)),
            pl.BlockSpec((ms, COL_TILE), sample_tile),
        ],
        out_specs=[
            pl.BlockSpec((tm, COL_TILE), lambda i, j: (i, j)),
            pl.BlockSpec((ms, COL_TILE), sample_tile),
        ],
        out_shape=[
            jax.ShapeDtypeStruct((m, n), F32),
            jax.ShapeDtypeStruct((ms, n), F32),
        ],
        scratch_shapes=[pltpu.VMEM((nj, k, COL_TILE), BF16)],
        compiler_params=_params("arbitrary", "arbitrary"),
        name="proj_res",
    )(a, a_s, w, b, res, res_s)


def _mlp_kernel(x_ref, xs_ref, g_ref, wu_ref, wd_ref, gf_ref, o_ref, os_ref, h_ref, hs_ref, *,
                final_norm):
    i = pl.program_id(0)
    f = pl.program_id(1)
    last_i = pl.num_programs(0) - 1
    last_f = pl.num_programs(1) - 1

    def start(x_r, h_r, o_r):
        x = x_r[...]
        h_r[...] = _rms_rows(x, g_ref[...]).astype(BF16)
        o_r[...] = x

    def accumulate(h_r, o_r):
        a = jnp.dot(h_r[...], wu_ref[...].astype(BF16), preferred_element_type=F32)
        a = jnp.maximum(a, 0.0)
        a = (a * a).astype(BF16)
        o_r[...] += jnp.dot(a, wd_ref[...].astype(BF16), preferred_element_type=F32)

    def finish(o_r):
        o_r[...] = _rms_rows(o_r[...], gf_ref[...])

    @pl.when(f == 0)
    def _():
        start(x_ref, h_ref, o_ref)

    accumulate(h_ref, o_ref)

    if final_norm:
        @pl.when(f == last_f)
        def _():
            finish(o_ref)

    @pl.when(i == last_i)
    def _():
        @pl.when(f == 0)
        def _():
            start(xs_ref, hs_ref, os_ref)

        accumulate(hs_ref, os_ref)

        if final_norm:
            @pl.when(f == last_f)
            def _():
                finish(os_ref)


def _mlp(x, xs, g, w_up, w_down, g_final, layer, final_norm):
    m = x.shape[0]
    ms = xs.shape[0]
    tm = min(ROW_TILE, m)
    return pl.pallas_call(
        functools.partial(_mlp_kernel, final_norm=final_norm),
        grid=(m // tm, D_FF // FF_TILE),
        in_specs=[
            pl.BlockSpec((tm, D_MODEL), lambda i, f: (i, 0)),
            pl.BlockSpec((ms, D_MODEL), lambda i, f: (0, 0)),
            pl.BlockSpec((None, 1, D_MODEL), lambda i, f: (layer, 0, 0)),
            pl.BlockSpec((None, D_MODEL, FF_TILE), lambda i, f: (layer, 0, f)),
            pl.BlockSpec((None, FF_TILE, D_MODEL), lambda i, f: (layer, f, 0)),
            pl.BlockSpec((1, D_MODEL), lambda i, f: (0, 0)),
        ],
        out_specs=[
            pl.BlockSpec((tm, D_MODEL), lambda i, f: (i, 0)),
            pl.BlockSpec((ms, D_MODEL), lambda i, f: (0, 0)),
        ],
        out_shape=[
            jax.ShapeDtypeStruct((m, D_MODEL), F32),
            jax.ShapeDtypeStruct((ms, D_MODEL), F32),
        ],
        scratch_shapes=[pltpu.VMEM((tm, D_MODEL), BF16), pltpu.VMEM((ms, D_MODEL), BF16)],
        compiler_params=pltpu.CompilerParams(dimension_semantics=("arbitrary", "arbitrary"),
                                             vmem_limit_bytes=MLP_VMEM_LIMIT_BYTES),
        name="mlp",
    )(x, xs, g, w_up, w_down, g_final)


def _glu_kernel(x_ref, xs_ref, g_ref, wa_ref, wg_ref, ba_ref, bg_ref, u_ref, us_ref,
                h_ref, hs_ref, wbf_ref):
    i = pl.program_id(0)
    j = pl.program_id(1)
    nj = D_MODEL // COL_TILE

    @pl.when(i == 0)
    def _():
        wbf_ref[j] = wa_ref[...].astype(BF16)
        wbf_ref[nj + j] = wg_ref[...].astype(BF16)

    def normalize(x_r, h_r):
        h_r[...] = _rms_rows(x_r[...], g_ref[...]).astype(BF16)

    wa, wg = wbf_ref[j], wbf_ref[nj + j]

    def project(h_r, rows):
        h = h_r[rows, :]
        return (jnp.dot(h, wa, preferred_element_type=F32) + ba_ref[...],
                jnp.dot(h, wg, preferred_element_type=F32) + bg_ref[...])

    @pl.when(j == 0)
    def _():
        normalize(x_ref, h_ref)

    half = h_ref.shape[0] // 2
    top, bottom = slice(0, half), slice(half, 2 * half)
    a_top, gate_top = project(h_ref, top)
    a_bottom, gate_bottom = project(h_ref, bottom)
    u_ref[top, :] = a_top * jax.nn.sigmoid(gate_top)
    u_ref[bottom, :] = a_bottom * jax.nn.sigmoid(gate_bottom)

    @pl.when(i == pl.num_programs(0) - 1)
    def _():
        @pl.when(j == 0)
        def _():
            normalize(xs_ref, hs_ref)

        a, gate = project(hs_ref, slice(0, hs_ref.shape[0]))
        us_ref[...] = a * jax.nn.sigmoid(gate)


def _glu(x, xs, g, w_in, b_in, layer, norm_layer):
    m, ms = x.shape[0], xs.shape[0]
    tm = min(ROW_TILE, m)
    ni = m // tm
    nj = D_MODEL // COL_TILE

    def gate_tile(i, j):
        return (layer, 0, nj + jnp.where(i == 0, j, nj - 1))

    return pl.pallas_call(
        _glu_kernel,
        grid=(ni, nj),
        in_specs=[
            pl.BlockSpec((tm, D_MODEL), lambda i, j: (i, 0), pipeline_mode=pl.Buffered(1)),
            pl.BlockSpec((ms, D_MODEL), lambda i, j: (0, 0)),
            pl.BlockSpec((None, 1, D_MODEL), lambda i, j: (norm_layer, 0, 0)),
            pl.BlockSpec((None, D_MODEL, COL_TILE), _first_pass_tile(layer, nj)),
            pl.BlockSpec((None, D_MODEL, COL_TILE), gate_tile),
            pl.BlockSpec((None, 1, COL_TILE), lambda i, j: (layer, 0, j)),
            pl.BlockSpec((None, 1, COL_TILE), lambda i, j: (layer, 0, j + nj)),
        ],
        out_specs=[
            pl.BlockSpec((tm, COL_TILE), lambda i, j: (i, j)),
            pl.BlockSpec((ms, COL_TILE), _last_tile_only(ni, lambda j: j)),
        ],
        out_shape=[
            jax.ShapeDtypeStruct((m, D_MODEL), F32),
            jax.ShapeDtypeStruct((ms, D_MODEL), F32),
        ],
        scratch_shapes=[pltpu.VMEM((tm, D_MODEL), BF16),
                        pltpu.VMEM((ms, D_MODEL), BF16),
                        pltpu.VMEM((2 * nj, D_MODEL, COL_TILE), BF16)],
        compiler_params=_params("arbitrary", "arbitrary"),
        name="glu",
    )(x, xs, g, w_in, w_in, b_in, b_in)


def _ln_swish(c, g, b):
    mu = jnp.mean(c, axis=-1, keepdims=True)
    xc = c - mu
    var = jnp.mean(xc * xc, axis=-1, keepdims=True)
    y = xc * lax.rsqrt(var + LN_EPS) * g + b
    return y * jax.nn.sigmoid(y)


def _conv_prompt_kernel(cur_ref, halo_ref, w_ref, bdw_ref, g_ref, b_ref, o_ref, win_ref, c_ref):
    t = pl.program_id(1)
    halo = halo_ref[...]
    win_ref[0:CONV_HALO, :] = jnp.where(t > 0, halo, jnp.zeros_like(halo))
    win_ref[CONV_HALO:, :] = cur_ref[...]
    off = CONV_HALO - (CONV_WIDTH - 1)
    n_win = CONV_HALO + CONV_ROWS
    for c in range(D_MODEL // LANES):
        cs = slice(c * LANES, (c + 1) * LANES)
        win = win_ref[:, cs]
        acc = jnp.zeros((CONV_ROWS, LANES), F32)
        for phase in range(SUBLANES):
            taps = [j for j in range(CONV_WIDTH) if (off + j) % SUBLANES == phase]
            if not taps:
                continue
            shifted = win if phase == 0 else pltpu.roll(win, n_win - phase, 0)
            for j in taps:
                base = (off + j) - phase
                acc = acc + shifted[base:base + CONV_ROWS] * w_ref[j:j + 1, cs]
        c_ref[:, cs] = acc + bdw_ref[:, cs]
    o_ref[...] = _ln_swish(c_ref[...], g_ref[...], b_ref[...]).astype(BF16)


def _conv_prompt(u, w_dw, b_dw, ln_g, ln_b, batch, seq, layer):
    nt = seq // CONV_ROWS
    ratio = CONV_ROWS // CONV_HALO
    vec = pl.BlockSpec((None, 1, D_MODEL), lambda b, t: (layer, 0, 0))
    return pl.pallas_call(
        _conv_prompt_kernel,
        grid=(batch, nt),
        in_specs=[
            pl.BlockSpec((CONV_ROWS, D_MODEL), lambda b, t: (b * nt + t, 0)),
            pl.BlockSpec((CONV_HALO, D_MODEL),
                         lambda b, t: (jnp.maximum((b * nt + t) * ratio - 1, 0), 0)),
            pl.BlockSpec((None, CONV_WIDTH, D_MODEL), lambda b, t: (layer, 0, 0)),
            vec, vec, vec,
        ],
        out_specs=pl.BlockSpec((CONV_ROWS, D_MODEL), lambda b, t: (b * nt + t, 0)),
        out_shape=jax.ShapeDtypeStruct((batch * seq, D_MODEL), BF16),
        scratch_shapes=[pltpu.VMEM((CONV_HALO + CONV_ROWS, D_MODEL), F32),
                        pltpu.VMEM((CONV_ROWS, D_MODEL), F32)],
        compiler_params=_params("parallel", "arbitrary"),
        name="conv_prompt",
    )(u, u, w_dw, b_dw, ln_g, ln_b)


CONV_SAMPLE_BATCH_BLOCK = 4


def _conv_sample_kernel(u_ref, w_ref, bdw_ref, g_ref, b_ref, o_ref, *, t_new):
    outs = []
    for bb in range(CONV_SAMPLE_BATCH_BLOCK):
        acc = jnp.zeros((t_new, D_MODEL), F32)
        for j in range(CONV_WIDTH):
            acc = acc + u_ref[bb, j:j + t_new, :] * w_ref[j:j + 1, :]
        outs.append(acc + bdw_ref[...])
    c = jnp.concatenate(outs, axis=0)
    o_ref[...] = _ln_swish(c, g_ref[...], b_ref[...]).astype(BF16)


def _conv_sample(u_pad, w_dw, b_dw, ln_g, ln_b, layer):
    batch, rows, _ = u_pad.shape
    t_new = rows - (CONV_WIDTH - 1)
    vec = pl.BlockSpec((None, 1, D_MODEL), lambda b: (layer, 0, 0))
    return pl.pallas_call(
        functools.partial(_conv_sample_kernel, t_new=t_new),
        grid=(batch // CONV_SAMPLE_BATCH_BLOCK,),
        in_specs=[
            pl.BlockSpec((CONV_SAMPLE_BATCH_BLOCK, rows, D_MODEL), lambda b: (b, 0, 0)),
            pl.BlockSpec((None, CONV_WIDTH, D_MODEL), lambda b: (layer, 0, 0)),
            vec, vec, vec,
        ],
        out_specs=pl.BlockSpec((CONV_SAMPLE_BATCH_BLOCK * t_new, D_MODEL), lambda b: (b, 0)),
        out_shape=jax.ShapeDtypeStruct((batch * t_new, D_MODEL), BF16),
        compiler_params=_params("parallel"),
        name="conv_sample",
    )(u_pad, w_dw, b_dw, ln_g, ln_b)


def _rope_tables(pos):
    inv = ROPE_THETA ** (-jnp.arange(0, HEAD_DIM, 2, dtype=F32) / HEAD_DIM)
    ang = pos.astype(F32)[:, None] * inv[None, :]
    cos, sin = jnp.cos(ang), jnp.sin(ang)
    reps = LANES // HEAD_DIM
    cos_t = jnp.tile(jnp.concatenate([cos, cos], axis=-1), (1, reps))
    sin_t = jnp.tile(jnp.concatenate([-sin, sin], axis=-1), (1, reps))
    scale = HEAD_DIM ** -0.5
    cos_all = jnp.concatenate([cos_t * scale, cos_t, jnp.ones_like(cos_t)], axis=1)
    sin_all = jnp.concatenate([sin_t * scale, sin_t, jnp.zeros_like(sin_t)], axis=1)
    return cos_all, sin_all


def kernel(x_prompt, x_sample, cache_k, cache_v, state_conv, norm_mix, norm_mlp, norm_final,
           attn_w_qkv, attn_b_qkv, attn_sinks, attn_w_o, attn_b_o,
           conv_w_in, conv_b_in, conv_w_dw, conv_b_dw, conv_ln_g, conv_ln_b,
           conv_w_out, conv_b_out, mlp_w_up, mlp_w_down):
    batch, seq, _ = x_prompt.shape
    dbatch, t_new, _ = x_sample.shape
    xp = x_prompt.reshape(batch * seq, D_MODEL)
    xs = x_sample.reshape(dbatch * t_new, D_MODEL)

    cos_p, sin_p = _rope_tables(jnp.arange(seq, dtype=jnp.int32))
    cos_s, sin_s = _rope_tables(PAST_LEN + jnp.arange(t_new, dtype=jnp.int32))
    cos_s = jnp.tile(cos_s, (dbatch, 1))
    sin_s = jnp.tile(sin_s, (dbatch, 1))

    norm_mix, norm_mlp = _layer_vec(norm_mix), _layer_vec(norm_mlp)
    attn_b_qkv, attn_b_o = _layer_vec(attn_b_qkv), _layer_vec(attn_b_o)
    conv_b_in, conv_b_out = _layer_vec(conv_b_in), _layer_vec(conv_b_out)
    conv_b_dw, conv_ln_g, conv_ln_b = _layer_vec(conv_b_dw), _layer_vec(conv_ln_g), _layer_vec(conv_ln_b)
    n_attn = cache_k.shape[0]
    ck = cache_k.reshape(n_attn, dbatch, WINDOW, KV_DIM)
    cv = cache_v.reshape(n_attn, dbatch, WINDOW, KV_DIM)
    sink_rows = jnp.repeat(attn_sinks, t_new, axis=1)[:, :, None]
    g_final = norm_final.reshape(1, D_MODEL)

    k_p, v_p, c_p, k_s, v_s, c_s = [], [], [], [], [], []
    for i in range(DEPTH):
        j = i // 2
        if i % 2 == 0:
            q, kv, q_s, kv_s = _qkv_rope(xp, xs, norm_mix, attn_w_qkv, attn_b_qkv,
                                         (cos_p, sin_p), (cos_s, sin_s), j, i)
            o = _attn_prompt(q, kv, attn_sinks, batch, seq, j)
            o_s = _attn_sample(q_s, kv_s, ck, cv, sink_rows, dbatch, t_new, j)
            xp, xs = _proj_res(o, o_s, attn_w_o, attn_b_o, xp, xs, j)
            kv3 = kv.reshape(batch, seq, 2 * KV_DIM)[:, seq - WINDOW:]
            k_p.append(kv3[..., :KV_DIM].reshape(batch, WINDOW, N_KV_HEADS, HEAD_DIM))
            v_p.append(kv3[..., KV_DIM:].reshape(batch, WINDOW, N_KV_HEADS, HEAD_DIM))
            kv3 = kv_s.reshape(dbatch, t_new, 2 * KV_DIM)
            k_new = kv3[..., :KV_DIM].reshape(dbatch, t_new, N_KV_HEADS, HEAD_DIM)
            v_new = kv3[..., KV_DIM:].reshape(dbatch, t_new, N_KV_HEADS, HEAD_DIM)
            k_s.append(jnp.concatenate([cache_k[j][:, t_new:], k_new], axis=1))
            v_s.append(jnp.concatenate([cache_v[j][:, t_new:], v_new], axis=1))
        else:
            conv_args = (conv_w_dw, conv_b_dw, conv_ln_g, conv_ln_b)
            u, u_s = _glu(xp, xs, norm_mix, conv_w_in, conv_b_in, j, i)
            c = _conv_prompt(u, *conv_args, batch, seq, j)
            u_pad = jnp.concatenate([state_conv[j], u_s.reshape(dbatch, t_new, D_MODEL)], axis=1)
            c_smp = _conv_sample(u_pad, *conv_args, j)
            xp, xs = _proj_res(c, c_smp, conv_w_out, conv_b_out, xp, xs, j)
            c_p.append(u.reshape(batch, seq, D_MODEL)[:, seq - (CONV_WIDTH - 1):])
            c_s.append(u_pad[:, t_new:])
        xp, xs = _mlp(xp, xs, norm_mlp, mlp_w_up, mlp_w_down, g_final, i, i == DEPTH - 1)

    y_p = xp.reshape(batch, seq, D_MODEL)
    y_s = xs.reshape(dbatch, t_new, D_MODEL)
    return (y_p, y_s, jnp.stack(k_p), jnp.stack(v_p), jnp.stack(c_p),
            jnp.stack(k_s), jnp.stack(v_s), jnp.stack(c_s))
```

```python
import functools

import jax
import jax.numpy as jnp
from jax import lax
from jax.experimental import pallas as pl
from jax.experimental.pallas import tpu as pltpu

D_MODEL = 2048
HEAD_DIM = 64
N_HEADS = 32
N_KV_HEADS = 8
GROUP = N_HEADS // N_KV_HEADS
Q_DIM = N_HEADS * HEAD_DIM
KV_DIM = N_KV_HEADS * HEAD_DIM
WINDOW = 128
PAST_LEN = 16384
ROPE_THETA = 10000.0
CONV_WIDTH = 31
D_FF = 4 * D_MODEL
RMS_EPS = 1e-6
LN_EPS = 1e-5
DEPTH = 4

VMEM_LIMIT_BYTES = 60 * 1024 * 1024
MLP_VMEM_LIMIT_BYTES = 62 * 1024 * 1024
LANES = 128
SUBLANES = 8
ROW_TILE = 1024
COL_TILE = 512
FF_TILE = 512
CONV_ROWS = 128
CONV_TILE_ROWS = 256
PROJ_SLAB = 256
CONV_HALO = 32
HEAD_SLAB = GROUP * HEAD_DIM

assert COL_TILE == KV_DIM and LANES == 2 * HEAD_DIM and HEAD_SLAB == 2 * LANES

BF16 = jnp.bfloat16
F32 = jnp.float32
NT_DIMS = (((1,), (1,)), ((), ()))


def _params(*sem):
    return pltpu.CompilerParams(dimension_semantics=sem, vmem_limit_bytes=VMEM_LIMIT_BYTES)


def _rms_rows(x, g):
    ms = jnp.mean(x * x, axis=-1, keepdims=True)
    return x * lax.rsqrt(ms + RMS_EPS) * g


def _layer_vec(v):
    return v.reshape(v.shape[0], 1, v.shape[1])


def _first_pass_tile(layer, n_tiles):
    def index_map(i, j):
        return (layer, 0, jnp.where(i == 0, j, n_tiles - 1))
    return index_map


def _last_tile_only(n_row_tiles, col_map):
    def index_map(i, j):
        return (0, jnp.where(i == n_row_tiles - 1, col_map(j), col_map(0)))
    return index_map


def _qkv_kernel(x_ref, xs_ref, g_ref, w_ref, b_ref, cos_ref, sin_ref, coss_ref, sins_ref,
                q_ref, kv_ref, qs_ref, kvs_ref, h_ref, hs_ref, wbf_ref):
    i = pl.program_id(0)
    j = pl.program_id(1)
    nq = Q_DIM // COL_TILE

    @pl.when(i == 0)
    def _():
        wbf_ref[j] = w_ref[...].astype(BF16)

    w = wbf_ref[j]
    bias = b_ref[...]
    reps = COL_TILE // LANES

    def normalize(x_r, h_r):
        h_r[...] = _rms_rows(x_r[...], g_ref[...]).astype(BF16)

    def project(h_r, rows):
        return jnp.dot(h_r[rows, :], w, preferred_element_type=F32) + bias

    def rotate_store(rows, y, cos_r, sin_r, q_r, kv_r):
        lane = lax.broadcasted_iota(jnp.int32, y.shape, 1)
        first_half = (lane % HEAD_DIM) < (HEAD_DIM // 2)
        rot = jnp.where(first_half,
                        pltpu.roll(y, COL_TILE - HEAD_DIM // 2, 1),
                        pltpu.roll(y, HEAD_DIM // 2, 1))
        r = (y * jnp.tile(cos_r[rows, :], (1, reps))
             + rot * jnp.tile(sin_r[rows, :], (1, reps)))

        @pl.when(j < nq)
        def _():
            q_r[rows, :] = r.astype(BF16)

        @pl.when(j >= nq)
        def _():
            kv_r[rows, :] = r

    @pl.when(j == 0)
    def _():
        normalize(x_ref, h_ref)

    half = h_ref.shape[0] // 2
    top, bottom = slice(0, half), slice(half, 2 * half)
    y_top = project(h_ref, top)
    y_bottom = project(h_ref, bottom)
    rotate_store(top, y_top, cos_ref, sin_ref, q_ref, kv_ref)
    rotate_store(bottom, y_bottom, cos_ref, sin_ref, q_ref, kv_ref)

    @pl.when(i == pl.num_programs(0) - 1)
    def _():
        @pl.when(j == 0)
        def _():
            normalize(xs_ref, hs_ref)

        rows = slice(0, hs_ref.shape[0])
        rotate_store(rows, project(hs_ref, rows), coss_ref, sins_ref, qs_ref, kvs_ref)


def _qkv_rope(x, xs, g, w, b, tabs, tabs_s, layer, norm_layer):
    m, ms = x.shape[0], xs.shape[0]
    tm = min(ROW_TILE, m)
    ni = m // tm
    cos_t, sin_t = tabs
    cos_s, sin_s = tabs_s
    n_tab = cos_t.shape[0] // tm
    nq = Q_DIM // COL_TILE
    nj = (Q_DIM + 2 * KV_DIM) // COL_TILE

    def tab_col(j):
        return jnp.maximum(j - nq + 1, 0)

    def q_col(j):
        return jnp.minimum(j, nq - 1)

    def kv_col(j):
        return jnp.maximum(j - nq, 0)

    def tab_map(i, j):
        return (i % n_tab, tab_col(j))

    return pl.pallas_call(
        _qkv_kernel,
        grid=(ni, nj),
        in_specs=[
            pl.BlockSpec((tm, D_MODEL), lambda i, j: (i, 0)),
            pl.BlockSpec((ms, D_MODEL), lambda i, j: (0, 0)),
            pl.BlockSpec((None, 1, D_MODEL), lambda i, j: (norm_layer, 0, 0)),
            pl.BlockSpec((None, D_MODEL, COL_TILE), _first_pass_tile(layer, nj)),
            pl.BlockSpec((None, 1, COL_TILE), lambda i, j: (layer, 0, j)),
            pl.BlockSpec((tm, LANES), tab_map),
            pl.BlockSpec((tm, LANES), tab_map),
            pl.BlockSpec((ms, LANES), _last_tile_only(ni, tab_col)),
            pl.BlockSpec((ms, LANES), _last_tile_only(ni, tab_col)),
        ],
        out_specs=[
            pl.BlockSpec((tm, COL_TILE), lambda i, j: (i, q_col(j))),
            pl.BlockSpec((tm, COL_TILE), lambda i, j: (i, kv_col(j))),
            pl.BlockSpec((ms, COL_TILE), _last_tile_only(ni, q_col)),
            pl.BlockSpec((ms, COL_TILE), _last_tile_only(ni, kv_col)),
        ],
        out_shape=[
            jax.ShapeDtypeStruct((m, Q_DIM), BF16),
            jax.ShapeDtypeStruct((m, 2 * KV_DIM), F32),
            jax.ShapeDtypeStruct((ms, Q_DIM), BF16),
            jax.ShapeDtypeStruct((ms, 2 * KV_DIM), F32),
        ],
        scratch_shapes=[pltpu.VMEM((tm, D_MODEL), BF16),
                        pltpu.VMEM((ms, D_MODEL), BF16),
                        pltpu.VMEM((nj, D_MODEL, COL_TILE), BF16)],
        compiler_params=_params("arbitrary", "arbitrary"),
        name="qkv_rope",
    )(x, xs, g, w, b, cos_t, sin_t, cos_s, sin_s)


def _rep_heads(x):
    n = x.shape[1]
    lane = lax.broadcasted_iota(jnp.int32, x.shape, 1)
    low = (lane % LANES) < HEAD_DIM
    even = jnp.where(low, x, pltpu.roll(x, HEAD_DIM, 1))
    odd = jnp.where(low, pltpu.roll(x, n - HEAD_DIM, 1), x)
    pieces = []
    for s in range(n // LANES):
        sl = slice(s * LANES, (s + 1) * LANES)
        pieces += [even[:, sl], even[:, sl], odd[:, sl], odd[:, sl]]
    return jnp.concatenate(pieces, axis=1)


def _attn_prompt_kernel(sinks_ref, q_ref, kc_ref, vc_ref, o_ref, kb_ref, vt_ref, ot_ref, *, layer):
    n = pl.program_id(1)

    @pl.when(n == 0)
    def _():
        kb_ref[0:WINDOW, :] = jnp.zeros((WINDOW, Q_DIM), BF16)
        vt_ref[:, 0:WINDOW] = jnp.zeros((KV_DIM, WINDOW), BF16)

    @pl.when(n > 0)
    def _():
        kb_ref[0:WINDOW, :] = kb_ref[WINDOW:, :]
        vt_ref[:, 0:WINDOW] = vt_ref[:, WINDOW:]

    kb_ref[WINDOW:, :] = _rep_heads(kc_ref[...]).astype(BF16)
    vt_ref[:, WINDOW:] = vc_ref[...].T.astype(BF16)

    cols = GROUP * WINDOW
    key = lax.broadcasted_iota(jnp.int32, (2 * WINDOW, cols), 0)
    qi = lax.broadcasted_iota(jnp.int32, (2 * WINDOW, cols), 1) % WINDOW
    allowed = jnp.logical_or(
        jnp.logical_and(key < WINDOW, jnp.logical_and(key >= qi, n > 0)),
        jnp.logical_and(key >= WINDOW, key - WINDOW <= qi))
    bias = jnp.where(allowed, 0.0, -jnp.inf).astype(F32)
    lane_group = lax.broadcasted_iota(jnp.int32, (WINDOW, HEAD_SLAB), 1) // HEAD_DIM
    col_group = lax.broadcasted_iota(jnp.int32, (1, cols), 1) // WINDOW

    def scores(h):
        hs = slice(h * HEAD_SLAB, (h + 1) * HEAD_SLAB)
        qh = q_ref[:, hs]
        zero = jnp.zeros_like(qh)
        qm = jnp.concatenate([jnp.where(lane_group == g, qh, zero) for g in range(GROUP)], axis=0)
        return lax.dot_general(kb_ref[:, hs], qm, NT_DIMS, preferred_element_type=F32) + bias

    def sink_row(h):
        sink = jnp.full((1, cols), sinks_ref[layer, h * GROUP + GROUP - 1], F32)
        for g in range(GROUP - 2, -1, -1):
            sink = jnp.where(col_group == g, sinks_ref[layer, h * GROUP + g], sink)
        return sink

    def softmax(s, sink):
        mx = jnp.maximum(jnp.max(s, axis=0, keepdims=True), sink)
        p = jnp.exp(s - mx)
        denom = jnp.sum(p, axis=0, keepdims=True) + jnp.exp(sink - mx)
        return p.astype(BF16), 1.0 / denom

    def finish(h, p, inv):
        ot = jnp.dot(vt_ref[h * HEAD_DIM:(h + 1) * HEAD_DIM, :], p, preferred_element_type=F32) * inv
        for g in range(GROUP):
            r0 = h * HEAD_SLAB + g * HEAD_DIM
            ot_ref[r0:r0 + HEAD_DIM, :] = ot[:, g * WINDOW:(g + 1) * WINDOW]

    s_next = scores(0)
    pending = None
    for h in range(N_KV_HEADS):
        s = s_next
        if h + 1 < N_KV_HEADS:
            s_next = scores(h + 1)
        p, inv = softmax(s, sink_row(h))
        if pending is not None:
            finish(*pending)
        pending = (h, p, inv)
    finish(*pending)
    o_ref[...] = ot_ref[...].T.astype(BF16)


def _attn_prompt(q, kv, sinks, batch, seq, layer):
    nb = seq // WINDOW
    return pl.pallas_call(
        functools.partial(_attn_prompt_kernel, layer=layer),
        grid=(batch, nb),
        in_specs=[
            pl.BlockSpec(memory_space=pltpu.SMEM),
            pl.BlockSpec((WINDOW, Q_DIM), lambda b, n: (b * nb + n, 0)),
            pl.BlockSpec((WINDOW, KV_DIM), lambda b, n: (b * nb + n, 0)),
            pl.BlockSpec((WINDOW, KV_DIM), lambda b, n: (b * nb + n, 1)),
        ],
        out_specs=pl.BlockSpec((WINDOW, Q_DIM), lambda b, n: (b * nb + n, 0)),
        out_shape=jax.ShapeDtypeStruct((batch * seq, Q_DIM), BF16),
        scratch_shapes=[pltpu.VMEM((2 * WINDOW, Q_DIM), BF16),
                        pltpu.VMEM((KV_DIM, 2 * WINDOW), BF16),
                        pltpu.VMEM((Q_DIM, WINDOW), F32)],
        compiler_params=_params("parallel", "arbitrary"),
        name="attn_prompt",
    )(sinks, q, kv, kv)


SAMPLE_BATCH_BLOCK = 4


def _attn_sample_kernel(sink_ref, q_ref, kvn_ref, ck_ref, cv_ref, o_ref, own_ref, *, t_new):
    pad = 2 * SUBLANES - t_new
    n_keys = WINDOW + t_new + pad
    rows = N_HEADS * t_new
    t_of_row = lax.broadcasted_iota(jnp.int32, (rows, n_keys), 0) % t_new
    col = lax.broadcasted_iota(jnp.int32, (rows, n_keys), 1)
    allowed = jnp.logical_or(
        jnp.logical_and(col < WINDOW, col >= t_of_row),
        jnp.logical_and(col >= WINDOW, col - WINDOW <= t_of_row))
    bias = jnp.where(allowed, 0.0, -jnp.inf).astype(F32)
    head_of_row = lax.broadcasted_iota(jnp.int32, (rows, Q_DIM), 0) // t_new
    head_of_lane = lax.broadcasted_iota(jnp.int32, (rows, Q_DIM), 1) // HEAD_DIM
    own_ref[...] = jnp.where(head_of_row == head_of_lane, 1.0, 0.0).astype(F32)
    sink = sink_ref[...]
    q_all = q_ref[...].astype(F32)
    kvn = kvn_ref[...]
    zpad = jnp.zeros((pad, KV_DIM), F32)

    def scores(bb):
        r0 = bb * t_new
        k_new = jnp.concatenate([kvn[r0:r0 + t_new, :KV_DIM], zpad], axis=0)
        k_all = jnp.concatenate([_rep_heads(ck_ref[bb]).astype(BF16),
                                 _rep_heads(k_new).astype(BF16)], axis=0)
        qf = (jnp.tile(q_all[r0:r0 + t_new], (N_HEADS, 1)) * own_ref[...]).astype(BF16)
        return lax.dot_general(qf, k_all, NT_DIMS, preferred_element_type=F32) + bias

    def softmax(s):
        mx = jnp.maximum(jnp.max(s, axis=-1, keepdims=True), sink)
        p = jnp.exp(s - mx)
        denom = jnp.sum(p, axis=-1, keepdims=True) + jnp.exp(sink - mx)
        return (p * (1.0 / denom)).astype(BF16)

    def finish(bb, pn):
        r0 = bb * t_new
        v_new = jnp.concatenate([kvn[r0:r0 + t_new, KV_DIM:], zpad], axis=0)
        v_all = jnp.concatenate([_rep_heads(cv_ref[bb]).astype(BF16),
                                 _rep_heads(v_new).astype(BF16)], axis=0)
        pv = jnp.dot(pn, v_all, preferred_element_type=F32) * own_ref[...]
        o = pv[0:t_new]
        for h in range(1, N_HEADS):
            o = o + pv[h * t_new:(h + 1) * t_new]
        o_ref[r0:r0 + t_new, :] = o.astype(BF16)

    s_next = scores(0)
    pending = None
    for bb in range(SAMPLE_BATCH_BLOCK):
        s = s_next
        if bb + 1 < SAMPLE_BATCH_BLOCK:
            s_next = scores(bb + 1)
        pn = softmax(s)
        if pending is not None:
            finish(*pending)
        pending = (bb, pn)
    finish(*pending)


def _attn_sample(q, kv, cache_k, cache_v, sink_rows, batch, t_new, layer):
    rows = SAMPLE_BATCH_BLOCK * t_new
    cache_spec = pl.BlockSpec((None, SAMPLE_BATCH_BLOCK, WINDOW, KV_DIM), lambda b: (layer, b, 0, 0))
    return pl.pallas_call(
        functools.partial(_attn_sample_kernel, t_new=t_new),
        grid=(batch // SAMPLE_BATCH_BLOCK,),
        in_specs=[
            pl.BlockSpec((None, N_HEADS * t_new, 1), lambda b: (layer, 0, 0)),
            pl.BlockSpec((rows, Q_DIM), lambda b: (b, 0)),
            pl.BlockSpec((rows, 2 * KV_DIM), lambda b: (b, 0)),
            cache_spec,
            cache_spec,
        ],
        out_specs=pl.BlockSpec((rows, Q_DIM), lambda b: (b, 0)),
        out_shape=jax.ShapeDtypeStruct((batch * t_new, Q_DIM), BF16),
        scratch_shapes=[pltpu.VMEM((N_HEADS * t_new, Q_DIM), F32)],
        compiler_params=_params("parallel"),
        name="attn_sample",
    )(sink_rows, q, kv, cache_k, cache_v)


def _proj_res_kernel(a_ref, as_ref, w_ref, b_ref, res_ref, ress_ref, o_ref, os_ref, wbf_ref):
    i = pl.program_id(0)
    j = pl.program_id(1)

    @pl.when(i == 0)
    def _():
        wbf_ref[j] = w_ref[...].astype(BF16)

    def project(a_r, res_r, o_r):
        y = jnp.dot(a_r[...], wbf_ref[j], preferred_element_type=F32)
        o_r[...] = res_r[...] + y + b_ref[...]

    project(a_ref, res_ref, o_ref)

    @pl.when(i == pl.num_programs(0) - 1)
    def _():
        project(as_ref, ress_ref, os_ref)


def _proj_res(a, a_s, w, b, res, res_s, layer):
    m, k = a.shape
    ms = a_s.shape[0]
    n = w.shape[-1]
    tm = min(ROW_TILE, m)
    ni = m // tm
    nj = n // COL_TILE
    sample_tile = _last_tile_only(ni, lambda j: j)
    return pl.pallas_call(
        _proj_res_kernel,
        grid=(ni, nj),
        in_specs=[
            pl.BlockSpec((tm, k), lambda i, j: (i, 0)),
            pl.BlockSpec((ms, k), lambda i, j: (0, 0)),
            pl.BlockSpec((None, k, COL_TILE), _first_pass_tile(layer, nj)),
            pl.BlockSpec((None, 1, COL_TILE), lambda i, j: (layer, 0, j)),
            pl.BlockSpec((tm, COL_TILE), lambda i, j: (i, j)),
            pl.BlockSpec((ms, COL_TILE), sample_tile),
        ],
        out_specs=[
            pl.BlockSpec((tm, COL_TILE), lambda i, j: (i, j)),
            pl.BlockSpec((ms, COL_TILE), sample_tile),
        ],
        out_shape=[
            jax.ShapeDtypeStruct((m, n), F32),
            jax.ShapeDtypeStruct((ms, n), F32),
        ],
        scratch_shapes=[pltpu.VMEM((nj, k, COL_TILE), BF16)],
        compiler_params=_params("arbitrary", "arbitrary"),
        name="proj_res",
    )(a, a_s, w, b, res, res_s)


def _mlp_kernel(x_ref, xs_ref, g_ref, wu_ref, wd_ref, gf_ref, o_ref, os_ref, h_ref, hs_ref, *,
                final_norm):
    i = pl.program_id(0)
    f = pl.program_id(1)
    last_i = pl.num_programs(0) - 1
    last_f = pl.num_programs(1) - 1

    def start(x_r, h_r, o_r):
        x = x_r[...]
        h_r[...] = _rms_rows(x, g_ref[...]).astype(BF16)
        o_r[...] = x

    def accumulate(h_r, o_r):
        a = jnp.dot(h_r[...], wu_ref[...].astype(BF16), preferred_element_type=F32)
        a = jnp.maximum(a, 0.0)
        a = (a * a).astype(BF16)
        o_r[...] += jnp.dot(a, wd_ref[...].astype(BF16), preferred_element_type=F32)

    def finish(o_r):
        o_r[...] = _rms_rows(o_r[...], gf_ref[...])

    @pl.when(f == 0)
    def _():
        start(x_ref, h_ref, o_ref)

    accumulate(h_ref, o_ref)

    if final_norm:
        @pl.when(f == last_f)
        def _():
            finish(o_ref)

    @pl.when(i == last_i)
    def _():
        @pl.when(f == 0)
        def _():
            start(xs_ref, hs_ref, os_ref)

        accumulate(hs_ref, os_ref)

        if final_norm:
            @pl.when(f == last_f)
            def _():
                finish(os_ref)


def _mlp(x, xs, g, w_up, w_down, g_final, layer, final_norm):
    m = x.shape[0]
    ms = xs.shape[0]
    tm = min(ROW_TILE, m)
    return pl.pallas_call(
        functools.partial(_mlp_kernel, final_norm=final_norm),
        grid=(m // tm, D_FF // FF_TILE),
        in_specs=[
            pl.BlockSpec((tm, D_MODEL), lambda i, f: (i, 0)),
            pl.BlockSpec((ms, D_MODEL), lambda i, f: (0, 0)),
            pl.BlockSpec((None, 1, D_MODEL), lambda i, f: (layer, 0, 0)),
            pl.BlockSpec((None, D_MODEL, FF_TILE), lambda i, f: (layer, 0, f)),
            pl.BlockSpec((None, FF_TILE, D_MODEL), lambda i, f: (layer, f, 0)),
            pl.BlockSpec((1, D_MODEL), lambda i, f: (0, 0)),
        ],
        out_specs=[
            pl.BlockSpec((tm, D_MODEL), lambda i, f: (i, 0)),
            pl.BlockSpec((ms, D_MODEL), lambda i, f: (0, 0)),
        ],
        out_shape=[
            jax.ShapeDtypeStruct((m, D_MODEL), F32),
            jax.ShapeDtypeStruct((ms, D_MODEL), F32),
        ],
        scratch_shapes=[pltpu.VMEM((tm, D_MODEL), BF16), pltpu.VMEM((ms, D_MODEL), BF16)],
        compiler_params=pltpu.CompilerParams(dimension_semantics=("arbitrary", "arbitrary"),
                                             vmem_limit_bytes=MLP_VMEM_LIMIT_BYTES),
        name="mlp",
    )(x, xs, g, w_up, w_down, g_final)


def _glu_kernel(x_ref, xs_ref, g_ref, wa_ref, wg_ref, ba_ref, bg_ref, u_ref, us_ref,
                h_ref, hs_ref, wbf_ref):
    i = pl.program_id(0)
    j = pl.program_id(1)
    nj = D_MODEL // COL_TILE

    @pl.when(i == 0)
    def _():
        wbf_ref[j] = wa_ref[...].astype(BF16)
        wbf_ref[nj + j] = wg_ref[...].astype(BF16)

    def normalize(x_r, h_r):
        h_r[...] = _rms_rows(x_r[...], g_ref[...]).astype(BF16)

    wa, wg = wbf_ref[j], wbf_ref[nj + j]

    def project(h_r, rows):
        h = h_r[rows, :]
        return (jnp.dot(h, wa, preferred_element_type=F32) + ba_ref[...],
                jnp.dot(h, wg, preferred_element_type=F32) + bg_ref[...])

    @pl.when(j == 0)
    def _():
        normalize(x_ref, h_ref)

    half = h_ref.shape[0] // 2
    top, bottom = slice(0, half), slice(half, 2 * half)
    a_top, gate_top = project(h_ref, top)
    a_bottom, gate_bottom = project(h_ref, bottom)
    u_ref[top, :] = a_top * jax.nn.sigmoid(gate_top)
    u_ref[bottom, :] = a_bottom * jax.nn.sigmoid(gate_bottom)

    @pl.when(i == pl.num_programs(0) - 1)
    def _():
        @pl.when(j == 0)
        def _():
            normalize(xs_ref, hs_ref)

        a, gate = project(hs_ref, slice(0, hs_ref.shape[0]))
        us_ref[...] = a * jax.nn.sigmoid(gate)


def _glu(x, xs, g, w_in, b_in, layer, norm_layer):
    m, ms = x.shape[0], xs.shape[0]
    tm = min(ROW_TILE, m)
    ni = m // tm
    nj = D_MODEL // COL_TILE

    def gate_tile(i, j):
        return (layer, 0, nj + jnp.where(i == 0, j, nj - 1))

    return pl.pallas_call(
        _glu_kernel,
        grid=(ni, nj),
        in_specs=[
            pl.BlockSpec((tm, D_MODEL), lambda i, j: (i, 0), pipeline_mode=pl.Buffered(1)),
            pl.BlockSpec((ms, D_MODEL), lambda i, j: (0, 0)),
            pl.BlockSpec((None, 1, D_MODEL), lambda i, j: (norm_layer, 0, 0)),
            pl.BlockSpec((None, D_MODEL, COL_TILE), _first_pass_tile(layer, nj)),
            pl.BlockSpec((None, D_MODEL, COL_TILE), gate_tile),
            pl.BlockSpec((None, 1, COL_TILE), lambda i, j: (layer, 0, j)),
            pl.BlockSpec((None, 1, COL_TILE), lambda i, j: (layer, 0, j + nj)),
        ],
        out_specs=[
            pl.BlockSpec((tm, COL_TILE), lambda i, j: (i, j)),
            pl.BlockSpec((ms, COL_TILE), _last_tile_only(ni, lambda j: j)),
        ],
        out_shape=[
            jax.ShapeDtypeStruct((m, D_MODEL), F32),
            jax.ShapeDtypeStruct((ms, D_MODEL), F32),
        ],
        scratch_shapes=[pltpu.VMEM((tm, D_MODEL), BF16),
                        pltpu.VMEM((ms, D_MODEL), BF16),
                        pltpu.VMEM((2 * nj, D_MODEL, COL_TILE), BF16)],
        compiler_params=_params("arbitrary", "arbitrary"),
        name="glu",
    )(x, xs, g, w_in, w_in, b_in, b_in)


def _ln_swish(c, g, b):
    mu = jnp.mean(c, axis=-1, keepdims=True)
    xc = c - mu
    var = jnp.mean(xc * xc, axis=-1, keepdims=True)
    y = xc * lax.rsqrt(var + LN_EPS) * g + b
    return y * jax.nn.sigmoid(y)


def _conv_lanes(win_ref, row0, w_ref, bdw_ref, c_ref, c):
    off = CONV_HALO - (CONV_WIDTH - 1)
    n_win = CONV_HALO + CONV_ROWS
    cs = slice(c * LANES, (c + 1) * LANES)
    win = win_ref[row0:row0 + n_win, cs]
    acc = jnp.zeros((CONV_ROWS, LANES), F32)
    for phase in range(SUBLANES):
        taps = [j for j in range(CONV_WIDTH) if (off + j) % SUBLANES == phase]
        if not taps:
            continue
        shifted = win if phase == 0 else pltpu.roll(win, n_win - phase, 0)
        for j in taps:
            base = (off + j) - phase
            acc = acc + shifted[base:base + CONV_ROWS] * w_ref[j:j + 1, cs]
    c_ref[:, cs] = acc + bdw_ref[:, cs]


def _conv_proj_kernel(cur_ref, halo_ref, wdw_ref, bdw_ref, lng_ref, lnb_ref, w_ref, b_ref, res_ref,
                      as_ref, ress_ref, o_ref, os_ref, win_ref, conv_ref, c_ref, wbf_ref, *,
                      tiles_per_seq):
    s = pl.program_id(0)
    n_tiles = pl.num_programs(0) - 1

    @pl.when(s == 0)
    def _():
        for c in range(D_MODEL // COL_TILE):
            cs = slice(c * COL_TILE, (c + 1) * COL_TILE)
            wbf_ref[:, cs] = w_ref[:, cs].astype(BF16)
        c_ref[...] = jnp.zeros(c_ref.shape, BF16)

    tile = jnp.minimum(s, n_tiles - 1)
    halo = halo_ref[...]
    win_ref[0:CONV_HALO, :] = jnp.where(tile % tiles_per_seq > 0, halo, jnp.zeros_like(halo))
    win_ref[CONV_HALO:, :] = cur_ref[...]

    def project(n):
        ns = slice(n * PROJ_SLAB, (n + 1) * PROJ_SLAB)
        y = jnp.dot(c_ref[(s + 1) % 2], wbf_ref[:, ns], preferred_element_type=F32)
        o_ref[:, ns] = res_ref[:, ns] + y + b_ref[:, ns]

    lane_groups = D_MODEL // LANES
    row_blocks = CONV_TILE_ROWS // CONV_ROWS
    n_slabs = D_MODEL // PROJ_SLAB
    every = (row_blocks * lane_groups) // n_slabs
    slot = s % 2
    for k in range(row_blocks):
        for c in range(lane_groups):
            idx = k * lane_groups + c
            if idx % every == 0:
                project(idx // every)
            _conv_lanes(win_ref, k * CONV_ROWS, wdw_ref, bdw_ref, conv_ref, c)
        c_ref[slot, k * CONV_ROWS:(k + 1) * CONV_ROWS, :] = _ln_swish(
            conv_ref[...], lng_ref[...], lnb_ref[...]).astype(BF16)

    @pl.when(s == n_tiles)
    def _():
        ys = jnp.dot(as_ref[...], wbf_ref[...], preferred_element_type=F32)
        os_ref[...] = ress_ref[...] + ys + b_ref[...]


def _conv_proj(u, a_s, res, res_s, w_dw, b_dw, ln_g, ln_b, w_out, b_out, seq, layer):
    m = u.shape[0]
    ms = a_s.shape[0]
    n_tiles = m // CONV_TILE_ROWS
    ratio = CONV_TILE_ROWS // CONV_HALO
    vec = pl.BlockSpec((None, 1, D_MODEL), lambda s: (layer, 0, 0))
    const = lambda shape: pl.BlockSpec(shape, lambda s: (0, 0))

    def cur_tile(s):
        return jnp.minimum(s, n_tiles - 1)

    def prev_tile(s):
        return jnp.maximum(s - 1, 0)

    return pl.pallas_call(
        functools.partial(_conv_proj_kernel, tiles_per_seq=seq // CONV_TILE_ROWS),
        grid=(n_tiles + 1,),
        in_specs=[
            pl.BlockSpec((CONV_TILE_ROWS, D_MODEL), lambda s: (cur_tile(s), 0)),
            pl.BlockSpec((CONV_HALO, D_MODEL), lambda s: (jnp.maximum(cur_tile(s) * ratio - 1, 0), 0)),
            pl.BlockSpec((None, CONV_WIDTH, D_MODEL), lambda s: (layer, 0, 0)),
            vec, vec, vec,
            pl.BlockSpec((None, D_MODEL, D_MODEL), lambda s: (layer, 0, 0)),
            vec,
            pl.BlockSpec((CONV_TILE_ROWS, D_MODEL), lambda s: (prev_tile(s), 0)),
            const((ms, D_MODEL)),
            const((ms, D_MODEL)),
        ],
        out_specs=[
            pl.BlockSpec((CONV_TILE_ROWS, D_MODEL), lambda s: (prev_tile(s), 0)),
            const((ms, D_MODEL)),
        ],
        out_shape=[
            jax.ShapeDtypeStruct((m, D_MODEL), F32),
            jax.ShapeDtypeStruct((ms, D_MODEL), F32),
        ],
        scratch_shapes=[pltpu.VMEM((CONV_HALO + CONV_TILE_ROWS, D_MODEL), F32),
                        pltpu.VMEM((CONV_ROWS, D_MODEL), F32),
                        pltpu.VMEM((2, CONV_TILE_ROWS, D_MODEL), BF16),
                        pltpu.VMEM((D_MODEL, D_MODEL), BF16)],
        compiler_params=_params("arbitrary"),
        name="conv_proj",
    )(u, u, w_dw, b_dw, ln_g, ln_b, w_out, b_out, res, a_s, res_s)


CONV_SAMPLE_BATCH_BLOCK = 4


def _conv_sample_kernel(u_ref, w_ref, bdw_ref, g_ref, b_ref, o_ref, *, t_new):
    outs = []
    for bb in range(CONV_SAMPLE_BATCH_BLOCK):
        acc = jnp.zeros((t_new, D_MODEL), F32)
        for j in range(CONV_WIDTH):
            acc = acc + u_ref[bb, j:j + t_new, :] * w_ref[j:j + 1, :]
        outs.append(acc + bdw_ref[...])
    c = jnp.concatenate(outs, axis=0)
    o_ref[...] = _ln_swish(c, g_ref[...], b_ref[...]).astype(BF16)


def _conv_sample(u_pad, w_dw, b_dw, ln_g, ln_b, layer):
    batch, rows, _ = u_pad.shape
    t_new = rows - (CONV_WIDTH - 1)
    vec = pl.BlockSpec((None, 1, D_MODEL), lambda b: (layer, 0, 0))
    return pl.pallas_call(
        functools.partial(_conv_sample_kernel, t_new=t_new),
        grid=(batch // CONV_SAMPLE_BATCH_BLOCK,),
        in_specs=[
            pl.BlockSpec((CONV_SAMPLE_BATCH_BLOCK, rows, D_MODEL), lambda b: (b, 0, 0)),
            pl.BlockSpec((None, CONV_WIDTH, D_MODEL), lambda b: (layer, 0, 0)),
            vec, vec, vec,
        ],
        out_specs=pl.BlockSpec((CONV_SAMPLE_BATCH_BLOCK * t_new, D_MODEL), lambda b: (b, 0)),
        out_shape=jax.ShapeDtypeStruct((batch * t_new, D_MODEL), BF16),
        compiler_params=_params("parallel"),
        name="conv_sample",
    )(u_pad, w_dw, b_dw, ln_g, ln_b)


def _rope_tables(pos):
    inv = ROPE_THETA ** (-jnp.arange(0, HEAD_DIM, 2, dtype=F32) / HEAD_DIM)
    ang = pos.astype(F32)[:, None] * inv[None, :]
    cos, sin = jnp.cos(ang), jnp.sin(ang)
    reps = LANES // HEAD_DIM
    cos_t = jnp.tile(jnp.concatenate([cos, cos], axis=-1), (1, reps))
    sin_t = jnp.tile(jnp.concatenate([-sin, sin], axis=-1), (1, reps))
    scale = HEAD_DIM ** -0.5
    cos_all = jnp.concatenate([cos_t * scale, cos_t, jnp.ones_like(cos_t)], axis=1)
    sin_all = jnp.concatenate([sin_t * scale, sin_t, jnp.zeros_like(sin_t)], axis=1)
    return cos_all, sin_all


def kernel(x_prompt, x_sample, cache_k, cache_v, state_conv, norm_mix, norm_mlp, norm_final,
           attn_w_qkv, attn_b_qkv, attn_sinks, attn_w_o, attn_b_o,
           conv_w_in, conv_b_in, conv_w_dw, conv_b_dw, conv_ln_g, conv_ln_b,
           conv_w_out, conv_b_out, mlp_w_up, mlp_w_down):
    batch, seq, _ = x_prompt.shape
    dbatch, t_new, _ = x_sample.shape
    xp = x_prompt.reshape(batch * seq, D_MODEL)
    xs = x_sample.reshape(dbatch * t_new, D_MODEL)

    cos_p, sin_p = _rope_tables(jnp.arange(seq, dtype=jnp.int32))
    cos_s, sin_s = _rope_tables(PAST_LEN + jnp.arange(t_new, dtype=jnp.int32))
    cos_s = jnp.tile(cos_s, (dbatch, 1))
    sin_s = jnp.tile(sin_s, (dbatch, 1))

    norm_mix, norm_mlp = _layer_vec(norm_mix), _layer_vec(norm_mlp)
    attn_b_qkv, attn_b_o = _layer_vec(attn_b_qkv), _layer_vec(attn_b_o)
    conv_b_in, conv_b_out = _layer_vec(conv_b_in), _layer_vec(conv_b_out)
    conv_b_dw, conv_ln_g, conv_ln_b = _layer_vec(conv_b_dw), _layer_vec(conv_ln_g), _layer_vec(conv_ln_b)
    n_attn = cache_k.shape[0]
    ck = cache_k.reshape(n_attn, dbatch, WINDOW, KV_DIM)
    cv = cache_v.reshape(n_attn, dbatch, WINDOW, KV_DIM)
    sink_rows = jnp.repeat(attn_sinks, t_new, axis=1)[:, :, None]
    g_final = norm_final.reshape(1, D_MODEL)

    k_p, v_p, c_p, k_s, v_s, c_s = [], [], [], [], [], []
    for i in range(DEPTH):
        j = i // 2
        if i % 2 == 0:
            q, kv, q_s, kv_s = _qkv_rope(xp, xs, norm_mix, attn_w_qkv, attn_b_qkv,
                                         (cos_p, sin_p), (cos_s, sin_s), j, i)
            o = _attn_prompt(q, kv, attn_sinks, batch, seq, j)
            o_s = _attn_sample(q_s, kv_s, ck, cv, sink_rows, dbatch, t_new, j)
            xp, xs = _proj_res(o, o_s, attn_w_o, attn_b_o, xp, xs, j)
            kv3 = kv.reshape(batch, seq, 2 * KV_DIM)[:, seq - WINDOW:]
            k_p.append(kv3[..., :KV_DIM].reshape(batch, WINDOW, N_KV_HEADS, HEAD_DIM))
            v_p.append(kv3[..., KV_DIM:].reshape(batch, WINDOW, N_KV_HEADS, HEAD_DIM))
            kv3 = kv_s.reshape(dbatch, t_new, 2 * KV_DIM)
            k_new = kv3[..., :KV_DIM].reshape(dbatch, t_new, N_KV_HEADS, HEAD_DIM)
            v_new = kv3[..., KV_DIM:].reshape(dbatch, t_new, N_KV_HEADS, HEAD_DIM)
            k_s.append(jnp.concatenate([cache_k[j][:, t_new:], k_new], axis=1))
            v_s.append(jnp.concatenate([cache_v[j][:, t_new:], v_new], axis=1))
        else:
            conv_args = (conv_w_dw, conv_b_dw, conv_ln_g, conv_ln_b)
            u, u_s = _glu(xp, xs, norm_mix, conv_w_in, conv_b_in, j, i)
            u_pad = jnp.concatenate([state_conv[j], u_s.reshape(dbatch, t_new, D_MODEL)], axis=1)
            c_smp = _conv_sample(u_pad, *conv_args, j)
            xp, xs = _conv_proj(u, c_smp, xp, xs, *conv_args, conv_w_out, conv_b_out, seq, j)
            c_p.append(u.reshape(batch, seq, D_MODEL)[:, seq - (CONV_WIDTH - 1):])
            c_s.append(u_pad[:, t_new:])
        xp, xs = _mlp(xp, xs, norm_mlp, mlp_w_up, mlp_w_down, g_final, i, i == DEPTH - 1)

    y_p = xp.reshape(batch, seq, D_MODEL)
    y_s = xs.reshape(dbatch, t_new, D_MODEL)
    return (y_p, y_s, jnp.stack(k_p), jnp.stack(v_p), jnp.stack(c_p),
            jnp.stack(k_s), jnp.stack(v_s), jnp.stack(c_s))
```

```python
import functools

import jax
import jax.numpy as jnp
from jax import lax
from jax.experimental import pallas as pl
from jax.experimental.pallas import tpu as pltpu

D_MODEL = 2048
HEAD_DIM = 64
N_HEADS = 32
N_KV_HEADS = 8
GROUP = N_HEADS // N_KV_HEADS
Q_DIM = N_HEADS * HEAD_DIM
KV_DIM = N_KV_HEADS * HEAD_DIM
WINDOW = 128
PAST_LEN = 16384
ROPE_THETA = 10000.0
CONV_WIDTH = 31
D_FF = 4 * D_MODEL
RMS_EPS = 1e-6
LN_EPS = 1e-5
DEPTH = 4

VMEM_LIMIT_BYTES = 60 * 1024 * 1024
MLP_VMEM_LIMIT_BYTES = 62 * 1024 * 1024
LANES = 128
SUBLANES = 8
ROW_TILE = 1024
COL_TILE = 512
FF_TILE = 512
CONV_ROWS = 128
CONV_TILE_ROWS = 256
PROJ_SLAB = 256
CONV_HALO = 32
HEAD_SLAB = GROUP * HEAD_DIM

assert COL_TILE == KV_DIM and LANES == 2 * HEAD_DIM and HEAD_SLAB == 2 * LANES

BF16 = jnp.bfloat16
F32 = jnp.float32
NT_DIMS = (((1,), (1,)), ((), ()))


def _params(*sem):
    return pltpu.CompilerParams(dimension_semantics=sem, vmem_limit_bytes=VMEM_LIMIT_BYTES)


def _rms_rows(x, g):
    ms = jnp.mean(x * x, axis=-1, keepdims=True)
    return x * lax.rsqrt(ms + RMS_EPS) * g


def _layer_vec(v):
    return v.reshape(v.shape[0], 1, v.shape[1])


def _first_pass_tile(layer, n_tiles):
    def index_map(i, j):
        return (layer, 0, jnp.where(i == 0, j, n_tiles - 1))
    return index_map


def _last_tile_only(n_row_tiles, col_map):
    def index_map(i, j):
        return (0, jnp.where(i == n_row_tiles - 1, col_map(j), col_map(0)))
    return index_map


def _qkv_kernel(x_ref, xs_ref, g_ref, w_ref, b_ref, cos_ref, sin_ref, coss_ref, sins_ref,
                q_ref, kv_ref, qs_ref, kvs_ref, h_ref, hs_ref, wbf_ref):
    i = pl.program_id(0)
    j = pl.program_id(1)
    nq = Q_DIM // COL_TILE

    @pl.when(i == 0)
    def _():
        wbf_ref[j] = w_ref[...].astype(BF16)

    w = wbf_ref[j]
    bias = b_ref[...]
    reps = COL_TILE // LANES

    def normalize(x_r, h_r):
        h_r[...] = _rms_rows(x_r[...], g_ref[...]).astype(BF16)

    def project(h_r, rows):
        return jnp.dot(h_r[rows, :], w, preferred_element_type=F32) + bias

    def rotate_store(rows, y, cos_r, sin_r, q_r, kv_r):
        lane = lax.broadcasted_iota(jnp.int32, y.shape, 1)
        first_half = (lane % HEAD_DIM) < (HEAD_DIM // 2)
        rot = jnp.where(first_half,
                        pltpu.roll(y, COL_TILE - HEAD_DIM // 2, 1),
                        pltpu.roll(y, HEAD_DIM // 2, 1))
        r = (y * jnp.tile(cos_r[rows, :], (1, reps))
             + rot * jnp.tile(sin_r[rows, :], (1, reps)))

        @pl.when(j < nq)
        def _():
            q_r[rows, :] = r.astype(BF16)

        @pl.when(j >= nq)
        def _():
            kv_r[rows, :] = r

    @pl.when(j == 0)
    def _():
        normalize(x_ref, h_ref)

    half = h_ref.shape[0] // 2
    top, bottom = slice(0, half), slice(half, 2 * half)
    y_top = project(h_ref, top)
    y_bottom = project(h_ref, bottom)
    rotate_store(top, y_top, cos_ref, sin_ref, q_ref, kv_ref)
    rotate_store(bottom, y_bottom, cos_ref, sin_ref, q_ref, kv_ref)

    @pl.when(i == pl.num_programs(0) - 1)
    def _():
        @pl.when(j == 0)
        def _():
            normalize(xs_ref, hs_ref)

        rows = slice(0, hs_ref.shape[0])
        rotate_store(rows, project(hs_ref, rows), coss_ref, sins_ref, qs_ref, kvs_ref)


def _qkv_rope(x, xs, g, w, b, tabs, tabs_s, layer, norm_layer):
    m, ms = x.shape[0], xs.shape[0]
    tm = min(ROW_TILE, m)
    ni = m // tm
    cos_t, sin_t = tabs
    cos_s, sin_s = tabs_s
    n_tab = cos_t.shape[0] // tm
    nq = Q_DIM // COL_TILE
    nj = (Q_DIM + 2 * KV_DIM) // COL_TILE

    def tab_col(j):
        return jnp.maximum(j - nq + 1, 0)

    def q_col(j):
        return jnp.minimum(j, nq - 1)

    def kv_col(j):
        return jnp.maximum(j - nq, 0)

    def tab_map(i, j):
        return (i % n_tab, tab_col(j))

    return pl.pallas_call(
        _qkv_kernel,
        grid=(ni, nj),
        in_specs=[
            pl.BlockSpec((tm, D_MODEL), lambda i, j: (i, 0)),
            pl.BlockSpec((ms, D_MODEL), lambda i, j: (0, 0)),
            pl.BlockSpec((None, 1, D_MODEL), lambda i, j: (norm_layer, 0, 0)),
            pl.BlockSpec((None, D_MODEL, COL_TILE), _first_pass_tile(layer, nj)),
            pl.BlockSpec((None, 1, COL_TILE), lambda i, j: (layer, 0, j)),
            pl.BlockSpec((tm, LANES), tab_map),
            pl.BlockSpec((tm, LANES), tab_map),
            pl.BlockSpec((ms, LANES), _last_tile_only(ni, tab_col)),
            pl.BlockSpec((ms, LANES), _last_tile_only(ni, tab_col)),
        ],
        out_specs=[
            pl.BlockSpec((tm, COL_TILE), lambda i, j: (i, q_col(j))),
            pl.BlockSpec((tm, COL_TILE), lambda i, j: (i, kv_col(j))),
            pl.BlockSpec((ms, COL_TILE), _last_tile_only(ni, q_col)),
            pl.BlockSpec((ms, COL_TILE), _last_tile_only(ni, kv_col)),
        ],
        out_shape=[
            jax.ShapeDtypeStruct((m, Q_DIM), BF16),
            jax.ShapeDtypeStruct((m, 2 * KV_DIM), F32),
            jax.ShapeDtypeStruct((ms, Q_DIM), BF16),
            jax.ShapeDtypeStruct((ms, 2 * KV_DIM), F32),
        ],
        scratch_shapes=[pltpu.VMEM((tm, D_MODEL), BF16),
                        pltpu.VMEM((ms, D_MODEL), BF16),
                        pltpu.VMEM((nj, D_MODEL, COL_TILE), BF16)],
        compiler_params=_params("arbitrary", "arbitrary"),
        name="qkv_rope",
    )(x, xs, g, w, b, cos_t, sin_t, cos_s, sin_s)


def _rep_heads(x):
    n = x.shape[1]
    lane = lax.broadcasted_iota(jnp.int32, x.shape, 1)
    low = (lane % LANES) < HEAD_DIM
    even = jnp.where(low, x, pltpu.roll(x, HEAD_DIM, 1))
    odd = jnp.where(low, pltpu.roll(x, n - HEAD_DIM, 1), x)
    pieces = []
    for s in range(n // LANES):
        sl = slice(s * LANES, (s + 1) * LANES)
        pieces += [even[:, sl], even[:, sl], odd[:, sl], odd[:, sl]]
    return jnp.concatenate(pieces, axis=1)


ATTN_SEQS = 2


def _attn_prompt_kernel(sinks_ref, q_ref, kc_ref, vc_ref, o_ref, kb_ref, vt_ref, ot_ref, *, layer):
    n = pl.program_id(1)

    @pl.when(n == 0)
    def _():
        kb_ref[:, 0:WINDOW, :] = jnp.zeros((ATTN_SEQS, WINDOW, Q_DIM), BF16)
        vt_ref[:, :, 0:WINDOW] = jnp.zeros((ATTN_SEQS, KV_DIM, WINDOW), BF16)

    @pl.when(n > 0)
    def _():
        kb_ref[:, 0:WINDOW, :] = kb_ref[:, WINDOW:, :]
        vt_ref[:, :, 0:WINDOW] = vt_ref[:, :, WINDOW:]

    for b in range(ATTN_SEQS):
        kb_ref[b, WINDOW:, :] = _rep_heads(kc_ref[b]).astype(BF16)
        vt_ref[b, :, WINDOW:] = vc_ref[b].T.astype(BF16)

    cols = GROUP * WINDOW
    key = lax.broadcasted_iota(jnp.int32, (2 * WINDOW, cols), 0)
    qi = lax.broadcasted_iota(jnp.int32, (2 * WINDOW, cols), 1) % WINDOW
    allowed = jnp.logical_or(
        jnp.logical_and(key < WINDOW, jnp.logical_and(key >= qi, n > 0)),
        jnp.logical_and(key >= WINDOW, key - WINDOW <= qi))
    bias = jnp.where(allowed, 0.0, -jnp.inf).astype(F32)
    lane_group = lax.broadcasted_iota(jnp.int32, (WINDOW, HEAD_SLAB), 1) // HEAD_DIM
    col_group = lax.broadcasted_iota(jnp.int32, (1, cols), 1) // WINDOW

    def scores(b, h):
        hs = slice(h * HEAD_SLAB, (h + 1) * HEAD_SLAB)
        qh = q_ref[b, :, hs]
        zero = jnp.zeros_like(qh)
        qm = jnp.concatenate([jnp.where(lane_group == g, qh, zero) for g in range(GROUP)], axis=0)
        return lax.dot_general(kb_ref[b, :, hs], qm, NT_DIMS, preferred_element_type=F32) + bias

    def sink_row(h):
        sink = jnp.full((1, cols), sinks_ref[layer, h * GROUP + GROUP - 1], F32)
        for g in range(GROUP - 2, -1, -1):
            sink = jnp.where(col_group == g, sinks_ref[layer, h * GROUP + g], sink)
        return sink

    def softmax(s, sink):
        mx = jnp.maximum(jnp.max(s, axis=0, keepdims=True), sink)
        p = jnp.exp(s - mx)
        denom = jnp.sum(p, axis=0, keepdims=True) + jnp.exp(sink - mx)
        return p.astype(BF16), 1.0 / denom

    def finish(b, h, p, inv):
        ot = jnp.dot(vt_ref[b, h * HEAD_DIM:(h + 1) * HEAD_DIM, :], p,
                     preferred_element_type=F32) * inv
        for g in range(GROUP):
            r0 = h * HEAD_SLAB + g * HEAD_DIM
            ot_ref[b, r0:r0 + HEAD_DIM, :] = ot[:, g * WINDOW:(g + 1) * WINDOW]

    work = [(b, h) for b in range(ATTN_SEQS) for h in range(N_KV_HEADS)]
    s_next = scores(*work[0])
    pending = None
    for idx, (b, h) in enumerate(work):
        s = s_next
        if idx + 1 < len(work):
            s_next = scores(*work[idx + 1])
        p, inv = softmax(s, sink_row(h))
        if pending is not None:
            finish(*pending)
        pending = (b, h, p, inv)
    finish(*pending)
    for b in range(ATTN_SEQS):
        o_ref[b] = ot_ref[b].T.astype(BF16)


def _attn_prompt(q, kv, sinks, batch, seq, layer):
    nb = seq // WINDOW
    q3 = q.reshape(batch, seq, Q_DIM)
    kv3 = kv.reshape(batch, seq, 2 * KV_DIM)
    out = pl.pallas_call(
        functools.partial(_attn_prompt_kernel, layer=layer),
        grid=(batch // ATTN_SEQS, nb),
        in_specs=[
            pl.BlockSpec(memory_space=pltpu.SMEM),
            pl.BlockSpec((ATTN_SEQS, WINDOW, Q_DIM), lambda b, n: (b, n, 0)),
            pl.BlockSpec((ATTN_SEQS, WINDOW, KV_DIM), lambda b, n: (b, n, 0)),
            pl.BlockSpec((ATTN_SEQS, WINDOW, KV_DIM), lambda b, n: (b, n, 1)),
        ],
        out_specs=pl.BlockSpec((ATTN_SEQS, WINDOW, Q_DIM), lambda b, n: (b, n, 0)),
        out_shape=jax.ShapeDtypeStruct((batch, seq, Q_DIM), BF16),
        scratch_shapes=[pltpu.VMEM((ATTN_SEQS, 2 * WINDOW, Q_DIM), BF16),
                        pltpu.VMEM((ATTN_SEQS, KV_DIM, 2 * WINDOW), BF16),
                        pltpu.VMEM((ATTN_SEQS, Q_DIM, WINDOW), F32)],
        compiler_params=_params("parallel", "arbitrary"),
        name="attn_prompt",
    )(sinks, q3, kv3, kv3)
    return out.reshape(batch * seq, Q_DIM)


SAMPLE_BATCH_BLOCK = 4


def _attn_sample_kernel(sink_ref, q_ref, kvn_ref, ck_ref, cv_ref, o_ref, own_ref, *, t_new):
    pad = 2 * SUBLANES - t_new
    n_keys = WINDOW + t_new + pad
    rows = N_HEADS * t_new
    t_of_row = lax.broadcasted_iota(jnp.int32, (rows, n_keys), 0) % t_new
    col = lax.broadcasted_iota(jnp.int32, (rows, n_keys), 1)
    allowed = jnp.logical_or(
        jnp.logical_and(col < WINDOW, col >= t_of_row),
        jnp.logical_and(col >= WINDOW, col - WINDOW <= t_of_row))
    bias = jnp.where(allowed, 0.0, -jnp.inf).astype(F32)
    head_of_row = lax.broadcasted_iota(jnp.int32, (rows, Q_DIM), 0) // t_new
    head_of_lane = lax.broadcasted_iota(jnp.int32, (rows, Q_DIM), 1) // HEAD_DIM
    own_ref[...] = jnp.where(head_of_row == head_of_lane, 1.0, 0.0).astype(F32)
    sink = sink_ref[...]
    q_all = q_ref[...].astype(F32)
    kvn = kvn_ref[...]
    zpad = jnp.zeros((pad, KV_DIM), F32)

    def scores(bb):
        r0 = bb * t_new
        k_new = jnp.concatenate([kvn[r0:r0 + t_new, :KV_DIM], zpad], axis=0)
        k_all = jnp.concatenate([_rep_heads(ck_ref[bb]).astype(BF16),
                                 _rep_heads(k_new).astype(BF16)], axis=0)
        qf = (jnp.tile(q_all[r0:r0 + t_new], (N_HEADS, 1)) * own_ref[...]).astype(BF16)
        return lax.dot_general(qf, k_all, NT_DIMS, preferred_element_type=F32) + bias

    def softmax(s):
        mx = jnp.maximum(jnp.max(s, axis=-1, keepdims=True), sink)
        p = jnp.exp(s - mx)
        denom = jnp.sum(p, axis=-1, keepdims=True) + jnp.exp(sink - mx)
        return (p * (1.0 / denom)).astype(BF16)

    def finish(bb, pn):
        r0 = bb * t_new
        v_new = jnp.concatenate([kvn[r0:r0 + t_new, KV_DIM:], zpad], axis=0)
        v_all = jnp.concatenate([_rep_heads(cv_ref[bb]).astype(BF16),
                                 _rep_heads(v_new).astype(BF16)], axis=0)
        pv = jnp.dot(pn, v_all, preferred_element_type=F32) * own_ref[...]
        o = pv[0:t_new]
        for h in range(1, N_HEADS):
            o = o + pv[h * t_new:(h + 1) * t_new]
        o_ref[r0:r0 + t_new, :] = o.astype(BF16)

    s_next = scores(0)
    pending = None
    for bb in range(SAMPLE_BATCH_BLOCK):
        s = s_next
        if bb + 1 < SAMPLE_BATCH_BLOCK:
            s_next = scores(bb + 1)
        pn = softmax(s)
        if pending is not None:
            finish(*pending)
        pending = (bb, pn)
    finish(*pending)


def _attn_sample(q, kv, cache_k, cache_v, sink_rows, batch, t_new, layer):
    rows = SAMPLE_BATCH_BLOCK * t_new
    cache_spec = pl.BlockSpec((None, SAMPLE_BATCH_BLOCK, WINDOW, KV_DIM), lambda b: (layer, b, 0, 0))
    return pl.pallas_call(
        functools.partial(_attn_sample_kernel, t_new=t_new),
        grid=(batch // SAMPLE_BATCH_BLOCK,),
        in_specs=[
            pl.BlockSpec((None, N_HEADS * t_new, 1), lambda b: (layer, 0, 0)),
            pl.BlockSpec((rows, Q_DIM), lambda b: (b, 0)),
            pl.BlockSpec((rows, 2 * KV_DIM), lambda b: (b, 0)),
            cache_spec,
            cache_spec,
        ],
        out_specs=pl.BlockSpec((rows, Q_DIM), lambda b: (b, 0)),
        out_shape=jax.ShapeDtypeStruct((batch * t_new, Q_DIM), BF16),
        scratch_shapes=[pltpu.VMEM((N_HEADS * t_new, Q_DIM), F32)],
        compiler_params=_params("parallel"),
        name="attn_sample",
    )(sink_rows, q, kv, cache_k, cache_v)


def _proj_res_kernel(a_ref, as_ref, w_ref, b_ref, res_ref, ress_ref, o_ref, os_ref, wbf_ref):
    i = pl.program_id(0)
    j = pl.program_id(1)

    @pl.when(i == 0)
    def _():
        wbf_ref[j] = w_ref[...].astype(BF16)

    def project(a_r, res_r, o_r):
        y = jnp.dot(a_r[...], wbf_ref[j], preferred_element_type=F32)
        o_r[...] = res_r[...] + y + b_ref[...]

    project(a_ref, res_ref, o_ref)

    @pl.when(i == pl.num_programs(0) - 1)
    def _():
        project(as_ref, ress_ref, os_ref)


def _proj_res(a, a_s, w, b, res, res_s, layer):
    m, k = a.shape
    ms = a_s.shape[0]
    n = w.shape[-1]
    tm = min(ROW_TILE, m)
    ni = m // tm
    nj = n // COL_TILE
    sample_tile = _last_tile_only(ni, lambda j: j)
    return pl.pallas_call(
        _proj_res_kernel,
        grid=(ni, nj),
        in_specs=[
            pl.BlockSpec((tm, k), lambda i, j: (i, 0)),
            pl.BlockSpec((ms, k), lambda i, j: (0, 0)),
            pl.BlockSpec((None, k, COL_TILE), _first_pass_tile(layer, nj)),
            pl.BlockSpec((None, 1, COL_TILE), lambda i, j: (layer, 0, j)),
            pl.BlockSpec((tm, COL_TILE), lambda i, j: (i, j)),
            pl.BlockSpec((ms, COL_TILE), sample_tile),
        ],
        out_specs=[
            pl.BlockSpec((tm, COL_TILE), lambda i, j: (i, j)),
            pl.BlockSpec((ms, COL_TILE), sample_tile),
        ],
        out_shape=[
            jax.ShapeDtypeStruct((m, n), F32),
            jax.ShapeDtypeStruct((ms, n), F32),
        ],
        scratch_shapes=[pltpu.VMEM((nj, k, COL_TILE), BF16)],
        compiler_params=_params("arbitrary", "arbitrary"),
        name="proj_res",
    )(a, a_s, w, b, res, res_s)


def _mlp_kernel(x_ref, xs_ref, g_ref, wu_ref, wd_ref, gf_ref, o_ref, os_ref, h_ref, hs_ref, *,
                final_norm):
    i = pl.program_id(0)
    f = pl.program_id(1)
    last_i = pl.num_programs(0) - 1
    last_f = pl.num_programs(1) - 1

    def start(x_r, h_r, o_r):
        x = x_r[...]
        h_r[...] = _rms_rows(x, g_ref[...]).astype(BF16)
        o_r[...] = x

    def accumulate(h_r, o_r):
        a = jnp.dot(h_r[...], wu_ref[...].astype(BF16), preferred_element_type=F32)
        a = jnp.maximum(a, 0.0)
        a = (a * a).astype(BF16)
        o_r[...] += jnp.dot(a, wd_ref[...].astype(BF16), preferred_element_type=F32)

    def finish(o_r):
        o_r[...] = _rms_rows(o_r[...], gf_ref[...])

    @pl.when(f == 0)
    def _():
        start(x_ref, h_ref, o_ref)

    accumulate(h_ref, o_ref)

    if final_norm:
        @pl.when(f == last_f)
        def _():
            finish(o_ref)

    @pl.when(i == last_i)
    def _():
        @pl.when(f == 0)
        def _():
            start(xs_ref, hs_ref, os_ref)

        accumulate(hs_ref, os_ref)

        if final_norm:
            @pl.when(f == last_f)
            def _():
                finish(os_ref)


def _mlp(x, xs, g, w_up, w_down, g_final, layer, final_norm):
    m = x.shape[0]
    ms = xs.shape[0]
    tm = min(ROW_TILE, m)
    return pl.pallas_call(
        functools.partial(_mlp_kernel, final_norm=final_norm),
        grid=(m // tm, D_FF // FF_TILE),
        in_specs=[
            pl.BlockSpec((tm, D_MODEL), lambda i, f: (i, 0)),
            pl.BlockSpec((ms, D_MODEL), lambda i, f: (0, 0)),
            pl.BlockSpec((None, 1, D_MODEL), lambda i, f: (layer, 0, 0)),
            pl.BlockSpec((None, D_MODEL, FF_TILE), lambda i, f: (layer, 0, f)),
            pl.BlockSpec((None, FF_TILE, D_MODEL), lambda i, f: (layer, f, 0)),
            pl.BlockSpec((1, D_MODEL), lambda i, f: (0, 0)),
        ],
        out_specs=[
            pl.BlockSpec((tm, D_MODEL), lambda i, f: (i, 0)),
            pl.BlockSpec((ms, D_MODEL), lambda i, f: (0, 0)),
        ],
        out_shape=[
            jax.ShapeDtypeStruct((m, D_MODEL), F32),
            jax.ShapeDtypeStruct((ms, D_MODEL), F32),
        ],
        scratch_shapes=[pltpu.VMEM((tm, D_MODEL), BF16), pltpu.VMEM((ms, D_MODEL), BF16)],
        compiler_params=pltpu.CompilerParams(dimension_semantics=("arbitrary", "arbitrary"),
                                             vmem_limit_bytes=MLP_VMEM_LIMIT_BYTES),
        name="mlp",
    )(x, xs, g, w_up, w_down, g_final)


def _glu_kernel(x_ref, xs_ref, g_ref, wa_ref, wg_ref, ba_ref, bg_ref, u_ref, us_ref,
                h_ref, hs_ref, wbf_ref):
    i = pl.program_id(0)
    j = pl.program_id(1)
    nj = D_MODEL // COL_TILE

    @pl.when(i == 0)
    def _():
        wbf_ref[j] = wa_ref[...].astype(BF16)
        wbf_ref[nj + j] = wg_ref[...].astype(BF16)

    def normalize(x_r, h_r):
        h_r[...] = _rms_rows(x_r[...], g_ref[...]).astype(BF16)

    wa, wg = wbf_ref[j], wbf_ref[nj + j]

    def project(h_r, rows):
        h = h_r[rows, :]
        return (jnp.dot(h, wa, preferred_element_type=F32) + ba_ref[...],
                jnp.dot(h, wg, preferred_element_type=F32) + bg_ref[...])

    @pl.when(j == 0)
    def _():
        normalize(x_ref, h_ref)

    half = h_ref.shape[0] // 2
    top, bottom = slice(0, half), slice(half, 2 * half)
    a_top, gate_top = project(h_ref, top)
    a_bottom, gate_bottom = project(h_ref, bottom)
    u_ref[top, :] = a_top * jax.nn.sigmoid(gate_top)
    u_ref[bottom, :] = a_bottom * jax.nn.sigmoid(gate_bottom)

    @pl.when(i == pl.num_programs(0) - 1)
    def _():
        @pl.when(j == 0)
        def _():
            normalize(xs_ref, hs_ref)

        a, gate = project(hs_ref, slice(0, hs_ref.shape[0]))
        us_ref[...] = a * jax.nn.sigmoid(gate)


def _glu(x, xs, g, w_in, b_in, layer, norm_layer):
    m, ms = x.shape[0], xs.shape[0]
    tm = min(ROW_TILE, m)
    ni = m // tm
    nj = D_MODEL // COL_TILE

    def gate_tile(i, j):
        return (layer, 0, nj + jnp.where(i == 0, j, nj - 1))

    return pl.pallas_call(
        _glu_kernel,
        grid=(ni, nj),
        in_specs=[
            pl.BlockSpec((tm, D_MODEL), lambda i, j: (i, 0), pipeline_mode=pl.Buffered(1)),
            pl.BlockSpec((ms, D_MODEL), lambda i, j: (0, 0)),
            pl.BlockSpec((None, 1, D_MODEL), lambda i, j: (norm_layer, 0, 0)),
            pl.BlockSpec((None, D_MODEL, COL_TILE), _first_pass_tile(layer, nj)),
            pl.BlockSpec((None, D_MODEL, COL_TILE), gate_tile),
            pl.BlockSpec((None, 1, COL_TILE), lambda i, j: (layer, 0, j)),
            pl.BlockSpec((None, 1, COL_TILE), lambda i, j: (layer, 0, j + nj)),
        ],
        out_specs=[
            pl.BlockSpec((tm, COL_TILE), lambda i, j: (i, j)),
            pl.BlockSpec((ms, COL_TILE), _last_tile_only(ni, lambda j: j)),
        ],
        out_shape=[
            jax.ShapeDtypeStruct((m, D_MODEL), F32),
            jax.ShapeDtypeStruct((ms, D_MODEL), F32),
        ],
        scratch_shapes=[pltpu.VMEM((tm, D_MODEL), BF16),
                        pltpu.VMEM((ms, D_MODEL), BF16),
                        pltpu.VMEM((2 * nj, D_MODEL, COL_TILE), BF16)],
        compiler_params=_params("arbitrary", "arbitrary"),
        name="glu",
    )(x, xs, g, w_in, w_in, b_in, b_in)


def _ln_swish(c, g, b):
    mu = jnp.mean(c, axis=-1, keepdims=True)
    xc = c - mu
    var = jnp.mean(xc * xc, axis=-1, keepdims=True)
    y = xc * lax.rsqrt(var + LN_EPS) * g + b
    return y * jax.nn.sigmoid(y)


def _conv_lanes(win_ref, row0, w_ref, bdw_ref, c_ref, c):
    off = CONV_HALO - (CONV_WIDTH - 1)
    n_win = CONV_HALO + CONV_ROWS
    cs = slice(c * LANES, (c + 1) * LANES)
    win = win_ref[row0:row0 + n_win, cs]
    acc = jnp.zeros((CONV_ROWS, LANES), F32)
    for phase in range(SUBLANES):
        taps = [j for j in range(CONV_WIDTH) if (off + j) % SUBLANES == phase]
        if not taps:
            continue
        shifted = win if phase == 0 else pltpu.roll(win, n_win - phase, 0)
        for j in taps:
            base = (off + j) - phase
            acc = acc + shifted[base:base + CONV_ROWS] * w_ref[j:j + 1, cs]
    c_ref[:, cs] = acc + bdw_ref[:, cs]


def _conv_proj_kernel(cur_ref, halo_ref, wdw_ref, bdw_ref, lng_ref, lnb_ref, w_ref, b_ref, res_ref,
                      as_ref, ress_ref, o_ref, os_ref, win_ref, conv_ref, c_ref, wbf_ref, *,
                      tiles_per_seq):
    s = pl.program_id(0)
    n_tiles = pl.num_programs(0) - 1

    @pl.when(s == 0)
    def _():
        for c in range(D_MODEL // COL_TILE):
            cs = slice(c * COL_TILE, (c + 1) * COL_TILE)
            wbf_ref[:, cs] = w_ref[:, cs].astype(BF16)
        c_ref[...] = jnp.zeros(c_ref.shape, BF16)

    tile = jnp.minimum(s, n_tiles - 1)
    halo = halo_ref[...]
    win_ref[0:CONV_HALO, :] = jnp.where(tile % tiles_per_seq > 0, halo, jnp.zeros_like(halo))
    win_ref[CONV_HALO:, :] = cur_ref[...]

    def project(n):
        ns = slice(n * PROJ_SLAB, (n + 1) * PROJ_SLAB)
        y = jnp.dot(c_ref[(s + 1) % 2], wbf_ref[:, ns], preferred_element_type=F32)
        o_ref[:, ns] = res_ref[:, ns] + y + b_ref[:, ns]

    lane_groups = D_MODEL // LANES
    row_blocks = CONV_TILE_ROWS // CONV_ROWS
    n_slabs = D_MODEL // PROJ_SLAB
    every = (row_blocks * lane_groups) // n_slabs
    slot = s % 2
    for k in range(row_blocks):
        for c in range(lane_groups):
            idx = k * lane_groups + c
            if idx % every == 0:
                project(idx // every)
            _conv_lanes(win_ref, k * CONV_ROWS, wdw_ref, bdw_ref, conv_ref, c)
        c_ref[slot, k * CONV_ROWS:(k + 1) * CONV_ROWS, :] = _ln_swish(
            conv_ref[...], lng_ref[...], lnb_ref[...]).astype(BF16)

    @pl.when(s == n_tiles)
    def _():
        ys = jnp.dot(as_ref[...], wbf_ref[...], preferred_element_type=F32)
        os_ref[...] = ress_ref[...] + ys + b_ref[...]


def _conv_proj(u, a_s, res, res_s, w_dw, b_dw, ln_g, ln_b, w_out, b_out, seq, layer):
    m = u.shape[0]
    ms = a_s.shape[0]
    n_tiles = m // CONV_TILE_ROWS
    ratio = CONV_TILE_ROWS // CONV_HALO
    vec = pl.BlockSpec((None, 1, D_MODEL), lambda s: (layer, 0, 0))
    const = lambda shape: pl.BlockSpec(shape, lambda s: (0, 0))

    def cur_tile(s):
        return jnp.minimum(s, n_tiles - 1)

    def prev_tile(s):
        return jnp.maximum(s - 1, 0)

    return pl.pallas_call(
        functools.partial(_conv_proj_kernel, tiles_per_seq=seq // CONV_TILE_ROWS),
        grid=(n_tiles + 1,),
        in_specs=[
            pl.BlockSpec((CONV_TILE_ROWS, D_MODEL), lambda s: (cur_tile(s), 0)),
            pl.BlockSpec((CONV_HALO, D_MODEL), lambda s: (jnp.maximum(cur_tile(s) * ratio - 1, 0), 0)),
            pl.BlockSpec((None, CONV_WIDTH, D_MODEL), lambda s: (layer, 0, 0)),
            vec, vec, vec,
            pl.BlockSpec((None, D_MODEL, D_MODEL), lambda s: (layer, 0, 0)),
            vec,
            pl.BlockSpec((CONV_TILE_ROWS, D_MODEL), lambda s: (prev_tile(s), 0)),
            const((ms, D_MODEL)),
            const((ms, D_MODEL)),
        ],
        out_specs=[
            pl.BlockSpec((CONV_TILE_ROWS, D_MODEL), lambda s: (prev_tile(s), 0)),
            const((ms, D_MODEL)),
        ],
        out_shape=[
            jax.ShapeDtypeStruct((m, D_MODEL), F32),
            jax.ShapeDtypeStruct((ms, D_MODEL), F32),
        ],
        scratch_shapes=[pltpu.VMEM((CONV_HALO + CONV_TILE_ROWS, D_MODEL), F32),
                        pltpu.VMEM((CONV_ROWS, D_MODEL), F32),
                        pltpu.VMEM((2, CONV_TILE_ROWS, D_MODEL), BF16),
                        pltpu.VMEM((D_MODEL, D_MODEL), BF16)],
        compiler_params=_params("arbitrary"),
        name="conv_proj",
    )(u, u, w_dw, b_dw, ln_g, ln_b, w_out, b_out, res, a_s, res_s)


CONV_SAMPLE_BATCH_BLOCK = 4


def _conv_sample_kernel(u_ref, w_ref, bdw_ref, g_ref, b_ref, o_ref, *, t_new):
    outs = []
    for bb in range(CONV_SAMPLE_BATCH_BLOCK):
        acc = jnp.zeros((t_new, D_MODEL), F32)
        for j in range(CONV_WIDTH):
            acc = acc + u_ref[bb, j:j + t_new, :] * w_ref[j:j + 1, :]
        outs.append(acc + bdw_ref[...])
    c = jnp.concatenate(outs, axis=0)
    o_ref[...] = _ln_swish(c, g_ref[...], b_ref[...]).astype(BF16)


def _conv_sample(u_pad, w_dw, b_dw, ln_g, ln_b, layer):
    batch, rows, _ = u_pad.shape
    t_new = rows - (CONV_WIDTH - 1)
    vec = pl.BlockSpec((None, 1, D_MODEL), lambda b: (layer, 0, 0))
    return pl.pallas_call(
        functools.partial(_conv_sample_kernel, t_new=t_new),
        grid=(batch // CONV_SAMPLE_BATCH_BLOCK,),
        in_specs=[
            pl.BlockSpec((CONV_SAMPLE_BATCH_BLOCK, rows, D_MODEL), lambda b: (b, 0, 0)),
            pl.BlockSpec((None, CONV_WIDTH, D_MODEL), lambda b: (layer, 0, 0)),
            vec, vec, vec,
        ],
        out_specs=pl.BlockSpec((CONV_SAMPLE_BATCH_BLOCK * t_new, D_MODEL), lambda b: (b, 0)),
        out_shape=jax.ShapeDtypeStruct((batch * t_new, D_MODEL), BF16),
        compiler_params=_params("parallel"),
        name="conv_sample",
    )(u_pad, w_dw, b_dw, ln_g, ln_b)


def _rope_tables(pos):
    inv = ROPE_THETA ** (-jnp.arange(0, HEAD_DIM, 2, dtype=F32) / HEAD_DIM)
    ang = pos.astype(F32)[:, None] * inv[None, :]
    cos, sin = jnp.cos(ang), jnp.sin(ang)
    reps = LANES // HEAD_DIM
    cos_t = jnp.tile(jnp.concatenate([cos, cos], axis=-1), (1, reps))
    sin_t = jnp.tile(jnp.concatenate([-sin, sin], axis=-1), (1, reps))
    scale = HEAD_DIM ** -0.5
    cos_all = jnp.concatenate([cos_t * scale, cos_t, jnp.ones_like(cos_t)], axis=1)
    sin_all = jnp.concatenate([sin_t * scale, sin_t, jnp.zeros_like(sin_t)], axis=1)
    return cos_all, sin_all


def kernel(x_prompt, x_sample, cache_k, cache_v, state_conv, norm_mix, norm_mlp, norm_final,
           attn_w_qkv, attn_b_qkv, attn_sinks, attn_w_o, attn_b_o,
           conv_w_in, conv_b_in, conv_w_dw, conv_b_dw, conv_ln_g, conv_ln_b,
           conv_w_out, conv_b_out, mlp_w_up, mlp_w_down):
    batch, seq, _ = x_prompt.shape
    dbatch, t_new, _ = x_sample.shape
    xp = x_prompt.reshape(batch * seq, D_MODEL)
    xs = x_sample.reshape(dbatch * t_new, D_MODEL)

    cos_p, sin_p = _rope_tables(jnp.arange(seq, dtype=jnp.int32))
    cos_s, sin_s = _rope_tables(PAST_LEN + jnp.arange(t_new, dtype=jnp.int32))
    cos_s = jnp.tile(cos_s, (dbatch, 1))
    sin_s = jnp.tile(sin_s, (dbatch, 1))

    norm_mix, norm_mlp = _layer_vec(norm_mix), _layer_vec(norm_mlp)
    attn_b_qkv, attn_b_o = _layer_vec(attn_b_qkv), _layer_vec(attn_b_o)
    conv_b_in, conv_b_out = _layer_vec(conv_b_in), _layer_vec(conv_b_out)
    conv_b_dw, conv_ln_g, conv_ln_b = _layer_vec(conv_b_dw), _layer_vec(conv_ln_g), _layer_vec(conv_ln_b)
    n_attn = cache_k.shape[0]
    ck = cache_k.reshape(n_attn, dbatch, WINDOW, KV_DIM)
    cv = cache_v.reshape(n_attn, dbatch, WINDOW, KV_DIM)
    sink_rows = jnp.repeat(attn_sinks, t_new, axis=1)[:, :, None]
    g_final = norm_final.reshape(1, D_MODEL)

    k_p, v_p, c_p, k_new, v_new, u_new = [], [], [], [], [], []
    for i in range(DEPTH):
        j = i // 2
        if i % 2 == 0:
            q, kv, q_s, kv_s = _qkv_rope(xp, xs, norm_mix, attn_w_qkv, attn_b_qkv,
                                         (cos_p, sin_p), (cos_s, sin_s), j, i)
            o = _attn_prompt(q, kv, attn_sinks, batch, seq, j)
            o_s = _attn_sample(q_s, kv_s, ck, cv, sink_rows, dbatch, t_new, j)
            xp, xs = _proj_res(o, o_s, attn_w_o, attn_b_o, xp, xs, j)
            kv3 = kv.reshape(batch, seq, 2 * KV_DIM)[:, seq - WINDOW:]
            k_p.append(kv3[..., :KV_DIM].reshape(batch, WINDOW, N_KV_HEADS, HEAD_DIM))
            v_p.append(kv3[..., KV_DIM:].reshape(batch, WINDOW, N_KV_HEADS, HEAD_DIM))
            kv3 = kv_s.reshape(dbatch, t_new, 2 * KV_DIM)
            k_new.append(kv3[..., :KV_DIM].reshape(dbatch, t_new, N_KV_HEADS, HEAD_DIM))
            v_new.append(kv3[..., KV_DIM:].reshape(dbatch, t_new, N_KV_HEADS, HEAD_DIM))
        else:
            conv_args = (conv_w_dw, conv_b_dw, conv_ln_g, conv_ln_b)
            u, u_s = _glu(xp, xs, norm_mix, conv_w_in, conv_b_in, j, i)
            u_s = u_s.reshape(dbatch, t_new, D_MODEL)
            c_smp = _conv_sample(jnp.concatenate([state_conv[j], u_s], axis=1), *conv_args, j)
            xp, xs = _conv_proj(u, c_smp, xp, xs, *conv_args, conv_w_out, conv_b_out, seq, j)
            c_p.append(u.reshape(batch, seq, D_MODEL)[:, seq - (CONV_WIDTH - 1):])
            u_new.append(u_s)
        xp, xs = _mlp(xp, xs, norm_mlp, mlp_w_up, mlp_w_down, g_final, i, i == DEPTH - 1)

    k_s = jnp.concatenate([cache_k[:, :, t_new:], jnp.stack(k_new)], axis=2)
    v_s = jnp.concatenate([cache_v[:, :, t_new:], jnp.stack(v_new)], axis=2)
    c_s = jnp.concatenate([state_conv[:, :, t_new:], jnp.stack(u_new)], axis=2)

    y_p = xp.reshape(batch, seq, D_MODEL)
    y_s = xs.reshape(dbatch, t_new, D_MODEL)
    return (y_p, y_s, jnp.stack(k_p), jnp.stack(v_p), jnp.stack(c_p), k_s, v_s, c_s)
```

```python
import functools

import jax
import jax.numpy as jnp
from jax import lax
from jax.experimental import pallas as pl
from jax.experimental.pallas import tpu as pltpu

D_MODEL = 2048
HEAD_DIM = 64
N_HEADS = 32
N_KV_HEADS = 8
GROUP = N_HEADS // N_KV_HEADS
Q_DIM = N_HEADS * HEAD_DIM
KV_DIM = N_KV_HEADS * HEAD_DIM
WINDOW = 128
PAST_LEN = 16384
ROPE_THETA = 10000.0
CONV_WIDTH = 31
D_FF = 4 * D_MODEL
RMS_EPS = 1e-6
LN_EPS = 1e-5
DEPTH = 4

VMEM_LIMIT_BYTES = 60 * 1024 * 1024
MLP_VMEM_LIMIT_BYTES = 62 * 1024 * 1024
LANES = 128
SUBLANES = 8
ROW_TILE = 1024
COL_TILE = 512
FF_TILE = 512
CONV_ROWS = 128
CONV_TILE_ROWS = 256
PROJ_SLAB = 256
CONV_HALO = 32
HEAD_SLAB = GROUP * HEAD_DIM

assert COL_TILE == KV_DIM and LANES == 2 * HEAD_DIM and HEAD_SLAB == 2 * LANES

BF16 = jnp.bfloat16
F32 = jnp.float32
NT_DIMS = (((1,), (1,)), ((), ()))


def _params(*sem):
    return pltpu.CompilerParams(dimension_semantics=sem, vmem_limit_bytes=VMEM_LIMIT_BYTES)


def _rms_rows(x, g):
    ms = jnp.mean(x * x, axis=-1, keepdims=True)
    return x * lax.rsqrt(ms + RMS_EPS) * g


def _layer_vec(v):
    return v.reshape(v.shape[0], 1, v.shape[1])


def _first_pass_tile(layer, n_tiles):
    def index_map(i, j):
        return (layer, 0, jnp.where(i == 0, j, n_tiles - 1))
    return index_map


def _last_tile_only(n_row_tiles, col_map):
    def index_map(i, j):
        return (0, jnp.where(i == n_row_tiles - 1, col_map(j), col_map(0)))
    return index_map


def _qkv_kernel(x_ref, xs_ref, g_ref, w_ref, b_ref, cos_ref, sin_ref, coss_ref, sins_ref,
                q_ref, kv_ref, qs_ref, kvs_ref, h_ref, hs_ref, wbf_ref):
    i = pl.program_id(0)
    j = pl.program_id(1)
    nq = Q_DIM // COL_TILE

    @pl.when(i == 0)
    def _():
        wbf_ref[j] = w_ref[...].astype(BF16)

    w = wbf_ref[j]
    bias = b_ref[...]
    reps = COL_TILE // LANES

    def normalize(x_r, h_r):
        h_r[...] = _rms_rows(x_r[...], g_ref[...]).astype(BF16)

    def project(h_r, rows):
        return jnp.dot(h_r[rows, :], w, preferred_element_type=F32) + bias

    def rotate_store(rows, y, cos_r, sin_r, q_r, kv_r):
        lane = lax.broadcasted_iota(jnp.int32, y.shape, 1)
        first_half = (lane % HEAD_DIM) < (HEAD_DIM // 2)
        rot = jnp.where(first_half,
                        pltpu.roll(y, COL_TILE - HEAD_DIM // 2, 1),
                        pltpu.roll(y, HEAD_DIM // 2, 1))
        r = (y * jnp.tile(cos_r[rows, :], (1, reps))
             + rot * jnp.tile(sin_r[rows, :], (1, reps)))

        @pl.when(j < nq)
        def _():
            q_r[rows, :] = r.astype(BF16)

        @pl.when(j >= nq)
        def _():
            kv_r[rows, :] = r

    @pl.when(j == 0)
    def _():
        normalize(x_ref, h_ref)

    half = h_ref.shape[0] // 2
    top, bottom = slice(0, half), slice(half, 2 * half)
    y_top = project(h_ref, top)
    y_bottom = project(h_ref, bottom)
    rotate_store(top, y_top, cos_ref, sin_ref, q_ref, kv_ref)
    rotate_store(bottom, y_bottom, cos_ref, sin_ref, q_ref, kv_ref)

    @pl.when(i == pl.num_programs(0) - 1)
    def _():
        @pl.when(j == 0)
        def _():
            normalize(xs_ref, hs_ref)

        rows = slice(0, hs_ref.shape[0])
        rotate_store(rows, project(hs_ref, rows), coss_ref, sins_ref, qs_ref, kvs_ref)


def _qkv_rope(x, xs, g, w, b, tabs, tabs_s, layer, norm_layer):
    m, ms = x.shape[0], xs.shape[0]
    tm = min(ROW_TILE, m)
    ni = m // tm
    cos_t, sin_t = tabs
    cos_s, sin_s = tabs_s
    n_tab = cos_t.shape[0] // tm
    nq = Q_DIM // COL_TILE
    nj = (Q_DIM + 2 * KV_DIM) // COL_TILE

    def tab_col(j):
        return jnp.maximum(j - nq + 1, 0)

    def q_col(j):
        return jnp.minimum(j, nq - 1)

    def kv_col(j):
        return jnp.maximum(j - nq, 0)

    def tab_map(i, j):
        return (i % n_tab, tab_col(j))

    return pl.pallas_call(
        _qkv_kernel,
        grid=(ni, nj),
        in_specs=[
            pl.BlockSpec((tm, D_MODEL), lambda i, j: (i, 0)),
            pl.BlockSpec((ms, D_MODEL), lambda i, j: (0, 0)),
            pl.BlockSpec((None, 1, D_MODEL), lambda i, j: (norm_layer, 0, 0)),
            pl.BlockSpec((None, D_MODEL, COL_TILE), _first_pass_tile(layer, nj)),
            pl.BlockSpec((None, 1, COL_TILE), lambda i, j: (layer, 0, j)),
            pl.BlockSpec((tm, LANES), tab_map),
            pl.BlockSpec((tm, LANES), tab_map),
            pl.BlockSpec((ms, LANES), _last_tile_only(ni, tab_col)),
            pl.BlockSpec((ms, LANES), _last_tile_only(ni, tab_col)),
        ],
        out_specs=[
            pl.BlockSpec((tm, COL_TILE), lambda i, j: (i, q_col(j))),
            pl.BlockSpec((tm, COL_TILE), lambda i, j: (i, kv_col(j))),
            pl.BlockSpec((ms, COL_TILE), _last_tile_only(ni, q_col)),
            pl.BlockSpec((ms, COL_TILE), _last_tile_only(ni, kv_col)),
        ],
        out_shape=[
            jax.ShapeDtypeStruct((m, Q_DIM), BF16),
            jax.ShapeDtypeStruct((m, 2 * KV_DIM), F32),
            jax.ShapeDtypeStruct((ms, Q_DIM), BF16),
            jax.ShapeDtypeStruct((ms, 2 * KV_DIM), F32),
        ],
        scratch_shapes=[pltpu.VMEM((tm, D_MODEL), BF16),
                        pltpu.VMEM((ms, D_MODEL), BF16),
                        pltpu.VMEM((nj, D_MODEL, COL_TILE), BF16)],
        compiler_params=_params("arbitrary", "arbitrary"),
        name="qkv_rope",
    )(x, xs, g, w, b, cos_t, sin_t, cos_s, sin_s)


def _rep_heads(x):
    n = x.shape[1]
    lane = lax.broadcasted_iota(jnp.int32, x.shape, 1)
    low = (lane % LANES) < HEAD_DIM
    even = jnp.where(low, x, pltpu.roll(x, HEAD_DIM, 1))
    odd = jnp.where(low, pltpu.roll(x, n - HEAD_DIM, 1), x)
    pieces = []
    for s in range(n // LANES):
        sl = slice(s * LANES, (s + 1) * LANES)
        pieces += [even[:, sl], even[:, sl], odd[:, sl], odd[:, sl]]
    return jnp.concatenate(pieces, axis=1)


ATTN_SEQS = 4


def _attn_prompt_kernel(sinks_ref, q_ref, kc_ref, vc_ref, o_ref, kb_ref, vt_ref, ot_ref, *, layer):
    n = pl.program_id(1)

    @pl.when(n == 0)
    def _():
        kb_ref[:, 0:WINDOW, :] = jnp.zeros((ATTN_SEQS, WINDOW, Q_DIM), BF16)
        vt_ref[:, :, 0:WINDOW] = jnp.zeros((ATTN_SEQS, KV_DIM, WINDOW), BF16)

    @pl.when(n > 0)
    def _():
        kb_ref[:, 0:WINDOW, :] = kb_ref[:, WINDOW:, :]
        vt_ref[:, :, 0:WINDOW] = vt_ref[:, :, WINDOW:]

    for b in range(ATTN_SEQS):
        kb_ref[b, WINDOW:, :] = _rep_heads(kc_ref[b]).astype(BF16)
        vt_ref[b, :, WINDOW:] = vc_ref[b].T.astype(BF16)

    cols = GROUP * WINDOW
    key = lax.broadcasted_iota(jnp.int32, (2 * WINDOW, cols), 0)
    qi = lax.broadcasted_iota(jnp.int32, (2 * WINDOW, cols), 1) % WINDOW
    allowed = jnp.logical_or(
        jnp.logical_and(key < WINDOW, jnp.logical_and(key >= qi, n > 0)),
        jnp.logical_and(key >= WINDOW, key - WINDOW <= qi))
    bias = jnp.where(allowed, 0.0, -jnp.inf).astype(F32)
    lane_group = lax.broadcasted_iota(jnp.int32, (WINDOW, HEAD_SLAB), 1) // HEAD_DIM
    col_group = lax.broadcasted_iota(jnp.int32, (1, cols), 1) // WINDOW

    def scores(b, h):
        hs = slice(h * HEAD_SLAB, (h + 1) * HEAD_SLAB)
        qh = q_ref[b, :, hs]
        zero = jnp.zeros_like(qh)
        qm = jnp.concatenate([jnp.where(lane_group == g, qh, zero) for g in range(GROUP)], axis=0)
        return lax.dot_general(kb_ref[b, :, hs], qm, NT_DIMS, preferred_element_type=F32) + bias

    def sink_row(h):
        sink = jnp.full((1, cols), sinks_ref[layer, h * GROUP + GROUP - 1], F32)
        for g in range(GROUP - 2, -1, -1):
            sink = jnp.where(col_group == g, sinks_ref[layer, h * GROUP + g], sink)
        return sink

    def softmax(s, sink):
        mx = jnp.maximum(jnp.max(s, axis=0, keepdims=True), sink)
        p = jnp.exp(s - mx)
        denom = jnp.sum(p, axis=0, keepdims=True) + jnp.exp(sink - mx)
        return p.astype(BF16), 1.0 / denom

    def finish(b, h, p, inv):
        ot = jnp.dot(vt_ref[b, h * HEAD_DIM:(h + 1) * HEAD_DIM, :], p,
                     preferred_element_type=F32) * inv
        for g in range(GROUP):
            r0 = h * HEAD_SLAB + g * HEAD_DIM
            ot_ref[b, r0:r0 + HEAD_DIM, :] = ot[:, g * WINDOW:(g + 1) * WINDOW]

    work = [(b, h) for b in range(ATTN_SEQS) for h in range(N_KV_HEADS)]
    s_next = scores(*work[0])
    pending = None
    for idx, (b, h) in enumerate(work):
        s = s_next
        if idx + 1 < len(work):
            s_next = scores(*work[idx + 1])
        p, inv = softmax(s, sink_row(h))
        if pending is not None:
            finish(*pending)
        pending = (b, h, p, inv)
    finish(*pending)
    for b in range(ATTN_SEQS):
        o_ref[b] = ot_ref[b].T.astype(BF16)


def _attn_prompt(q, kv, sinks, batch, seq, layer):
    nb = seq // WINDOW
    q3 = q.reshape(batch, seq, Q_DIM)
    kv3 = kv.reshape(batch, seq, 2 * KV_DIM)
    out = pl.pallas_call(
        functools.partial(_attn_prompt_kernel, layer=layer),
        grid=(batch // ATTN_SEQS, nb),
        in_specs=[
            pl.BlockSpec(memory_space=pltpu.SMEM),
            pl.BlockSpec((ATTN_SEQS, WINDOW, Q_DIM), lambda b, n: (b, n, 0)),
            pl.BlockSpec((ATTN_SEQS, WINDOW, KV_DIM), lambda b, n: (b, n, 0)),
            pl.BlockSpec((ATTN_SEQS, WINDOW, KV_DIM), lambda b, n: (b, n, 1)),
        ],
        out_specs=pl.BlockSpec((ATTN_SEQS, WINDOW, Q_DIM), lambda b, n: (b, n, 0)),
        out_shape=jax.ShapeDtypeStruct((batch, seq, Q_DIM), BF16),
        scratch_shapes=[pltpu.VMEM((ATTN_SEQS, 2 * WINDOW, Q_DIM), BF16),
                        pltpu.VMEM((ATTN_SEQS, KV_DIM, 2 * WINDOW), BF16),
                        pltpu.VMEM((ATTN_SEQS, Q_DIM, WINDOW), F32)],
        compiler_params=_params("parallel", "arbitrary"),
        name="attn_prompt",
    )(sinks, q3, kv3, kv3)
    return out.reshape(batch * seq, Q_DIM)


SAMPLE_BATCH_BLOCK = 8


def _attn_sample_kernel(sink_ref, q_ref, kvn_ref, ck_ref, cv_ref, o_ref, own_ref, *, t_new):
    pad = 2 * SUBLANES - t_new
    n_keys = WINDOW + t_new + pad
    rows = N_HEADS * t_new
    t_of_row = lax.broadcasted_iota(jnp.int32, (rows, n_keys), 0) % t_new
    col = lax.broadcasted_iota(jnp.int32, (rows, n_keys), 1)
    allowed = jnp.logical_or(
        jnp.logical_and(col < WINDOW, col >= t_of_row),
        jnp.logical_and(col >= WINDOW, col - WINDOW <= t_of_row))
    bias = jnp.where(allowed, 0.0, -jnp.inf).astype(F32)
    head_of_row = lax.broadcasted_iota(jnp.int32, (rows, Q_DIM), 0) // t_new
    head_of_lane = lax.broadcasted_iota(jnp.int32, (rows, Q_DIM), 1) // HEAD_DIM
    own_ref[...] = jnp.where(head_of_row == head_of_lane, 1.0, 0.0).astype(F32)
    sink = sink_ref[...]
    q_all = q_ref[...].astype(F32)
    kvn = kvn_ref[...]
    zpad = jnp.zeros((pad, KV_DIM), F32)

    def scores(bb):
        r0 = bb * t_new
        k_new = jnp.concatenate([kvn[r0:r0 + t_new, :KV_DIM], zpad], axis=0)
        k_all = jnp.concatenate([_rep_heads(ck_ref[bb]).astype(BF16),
                                 _rep_heads(k_new).astype(BF16)], axis=0)
        qf = (jnp.tile(q_all[r0:r0 + t_new], (N_HEADS, 1)) * own_ref[...]).astype(BF16)
        return lax.dot_general(qf, k_all, NT_DIMS, preferred_element_type=F32) + bias

    def softmax(s):
        mx = jnp.maximum(jnp.max(s, axis=-1, keepdims=True), sink)
        p = jnp.exp(s - mx)
        denom = jnp.sum(p, axis=-1, keepdims=True) + jnp.exp(sink - mx)
        return (p * (1.0 / denom)).astype(BF16)

    def finish(bb, pn):
        r0 = bb * t_new
        v_new = jnp.concatenate([kvn[r0:r0 + t_new, KV_DIM:], zpad], axis=0)
        v_all = jnp.concatenate([_rep_heads(cv_ref[bb]).astype(BF16),
                                 _rep_heads(v_new).astype(BF16)], axis=0)
        pv = jnp.dot(pn, v_all, preferred_element_type=F32) * own_ref[...]
        o = pv[0:t_new]
        for h in range(1, N_HEADS):
            o = o + pv[h * t_new:(h + 1) * t_new]
        o_ref[r0:r0 + t_new, :] = o.astype(BF16)

    s_next = scores(0)
    pending = None
    for bb in range(SAMPLE_BATCH_BLOCK):
        s = s_next
        if bb + 1 < SAMPLE_BATCH_BLOCK:
            s_next = scores(bb + 1)
        pn = softmax(s)
        if pending is not None:
            finish(*pending)
        pending = (bb, pn)
    finish(*pending)


def _attn_sample(q, kv, cache_k, cache_v, sink_rows, batch, t_new, layer):
    rows = SAMPLE_BATCH_BLOCK * t_new
    cache_spec = pl.BlockSpec((None, SAMPLE_BATCH_BLOCK, WINDOW, KV_DIM), lambda b: (layer, b, 0, 0))
    return pl.pallas_call(
        functools.partial(_attn_sample_kernel, t_new=t_new),
        grid=(batch // SAMPLE_BATCH_BLOCK,),
        in_specs=[
            pl.BlockSpec((None, N_HEADS * t_new, 1), lambda b: (layer, 0, 0)),
            pl.BlockSpec((rows, Q_DIM), lambda b: (b, 0)),
            pl.BlockSpec((rows, 2 * KV_DIM), lambda b: (b, 0)),
            cache_spec,
            cache_spec,
        ],
        out_specs=pl.BlockSpec((rows, Q_DIM), lambda b: (b, 0)),
        out_shape=jax.ShapeDtypeStruct((batch * t_new, Q_DIM), BF16),
        scratch_shapes=[pltpu.VMEM((N_HEADS * t_new, Q_DIM), F32)],
        compiler_params=_params("parallel"),
        name="attn_sample",
    )(sink_rows, q, kv, cache_k, cache_v)


def _proj_res_kernel(a_ref, as_ref, w_ref, b_ref, res_ref, ress_ref, o_ref, os_ref, wbf_ref):
    i = pl.program_id(0)
    j = pl.program_id(1)

    @pl.when(i == 0)
    def _():
        wbf_ref[j] = w_ref[...].astype(BF16)

    def project(a_r, res_r, o_r):
        y = jnp.dot(a_r[...], wbf_ref[j], preferred_element_type=F32)
        o_r[...] = res_r[...] + y + b_ref[...]

    project(a_ref, res_ref, o_ref)

    @pl.when(i == pl.num_programs(0) - 1)
    def _():
        project(as_ref, ress_ref, os_ref)


def _proj_res(a, a_s, w, b, res, res_s, layer):
    m, k = a.shape
    ms = a_s.shape[0]
    n = w.shape[-1]
    tm = min(ROW_TILE, m)
    ni = m // tm
    nj = n // COL_TILE
    sample_tile = _last_tile_only(ni, lambda j: j)
    return pl.pallas_call(
        _proj_res_kernel,
        grid=(ni, nj),
        in_specs=[
            pl.BlockSpec((tm, k), lambda i, j: (i, 0)),
            pl.BlockSpec((ms, k), lambda i, j: (0, 0)),
            pl.BlockSpec((None, k, COL_TILE), _first_pass_tile(layer, nj)),
            pl.BlockSpec((None, 1, COL_TILE), lambda i, j: (layer, 0, j)),
            pl.BlockSpec((tm, COL_TILE), lambda i, j: (i, j)),
            pl.BlockSpec((ms, COL_TILE), sample_tile),
        ],
        out_specs=[
            pl.BlockSpec((tm, COL_TILE), lambda i, j: (i, j)),
            pl.BlockSpec((ms, COL_TILE), sample_tile),
        ],
        out_shape=[
            jax.ShapeDtypeStruct((m, n), F32),
            jax.ShapeDtypeStruct((ms, n), F32),
        ],
        scratch_shapes=[pltpu.VMEM((nj, k, COL_TILE), BF16)],
        compiler_params=_params("arbitrary", "arbitrary"),
        name="proj_res",
    )(a, a_s, w, b, res, res_s)


def _mlp_kernel(x_ref, xs_ref, g_ref, wu_ref, wd_ref, gf_ref, o_ref, os_ref, h_ref, hs_ref, *,
                final_norm):
    i = pl.program_id(0)
    f = pl.program_id(1)
    last_i = pl.num_programs(0) - 1
    last_f = pl.num_programs(1) - 1

    def start(x_r, h_r, o_r):
        x = x_r[...]
        h_r[...] = _rms_rows(x, g_ref[...]).astype(BF16)
        o_r[...] = x

    def accumulate(h_r, o_r):
        a = jnp.dot(h_r[...], wu_ref[...].astype(BF16), preferred_element_type=F32)
        a = jnp.maximum(a, 0.0)
        a = (a * a).astype(BF16)
        o_r[...] += jnp.dot(a, wd_ref[...].astype(BF16), preferred_element_type=F32)

    def finish(o_r):
        o_r[...] = _rms_rows(o_r[...], gf_ref[...])

    @pl.when(f == 0)
    def _():
        start(x_ref, h_ref, o_ref)

    accumulate(h_ref, o_ref)

    if final_norm:
        @pl.when(f == last_f)
        def _():
            finish(o_ref)

    @pl.when(i == last_i)
    def _():
        @pl.when(f == 0)
        def _():
            start(xs_ref, hs_ref, os_ref)

        accumulate(hs_ref, os_ref)

        if final_norm:
            @pl.when(f == last_f)
            def _():
                finish(os_ref)


def _mlp(x, xs, g, w_up, w_down, g_final, layer, final_norm):
    m = x.shape[0]
    ms = xs.shape[0]
    tm = min(ROW_TILE, m)
    return pl.pallas_call(
        functools.partial(_mlp_kernel, final_norm=final_norm),
        grid=(m // tm, D_FF // FF_TILE),
        in_specs=[
            pl.BlockSpec((tm, D_MODEL), lambda i, f: (i, 0)),
            pl.BlockSpec((ms, D_MODEL), lambda i, f: (0, 0)),
            pl.BlockSpec((None, 1, D_MODEL), lambda i, f: (layer, 0, 0)),
            pl.BlockSpec((None, D_MODEL, FF_TILE), lambda i, f: (layer, 0, f)),
            pl.BlockSpec((None, FF_TILE, D_MODEL), lambda i, f: (layer, f, 0)),
            pl.BlockSpec((1, D_MODEL), lambda i, f: (0, 0)),
        ],
        out_specs=[
            pl.BlockSpec((tm, D_MODEL), lambda i, f: (i, 0)),
            pl.BlockSpec((ms, D_MODEL), lambda i, f: (0, 0)),
        ],
        out_shape=[
            jax.ShapeDtypeStruct((m, D_MODEL), F32),
            jax.ShapeDtypeStruct((ms, D_MODEL), F32),
        ],
        scratch_shapes=[pltpu.VMEM((tm, D_MODEL), BF16), pltpu.VMEM((ms, D_MODEL), BF16)],
        compiler_params=pltpu.CompilerParams(dimension_semantics=("arbitrary", "arbitrary"),
                                             vmem_limit_bytes=MLP_VMEM_LIMIT_BYTES),
        name="mlp",
    )(x, xs, g, w_up, w_down, g_final)


def _glu_kernel(x_ref, xs_ref, g_ref, wa_ref, wg_ref, ba_ref, bg_ref, u_ref, us_ref,
                h_ref, hs_ref, wbf_ref):
    i = pl.program_id(0)
    j = pl.program_id(1)
    nj = D_MODEL // COL_TILE

    @pl.when(i == 0)
    def _():
        wbf_ref[j] = wa_ref[...].astype(BF16)
        wbf_ref[nj + j] = wg_ref[...].astype(BF16)

    def normalize(x_r, h_r):
        h_r[...] = _rms_rows(x_r[...], g_ref[...]).astype(BF16)

    wa, wg = wbf_ref[j], wbf_ref[nj + j]

    def project(h_r, rows):
        h = h_r[rows, :]
        return (jnp.dot(h, wa, preferred_element_type=F32) + ba_ref[...],
                jnp.dot(h, wg, preferred_element_type=F32) + bg_ref[...])

    @pl.when(j == 0)
    def _():
        normalize(x_ref, h_ref)

    half = h_ref.shape[0] // 2
    top, bottom = slice(0, half), slice(half, 2 * half)
    a_top, gate_top = project(h_ref, top)
    a_bottom, gate_bottom = project(h_ref, bottom)
    u_ref[top, :] = a_top * jax.nn.sigmoid(gate_top)
    u_ref[bottom, :] = a_bottom * jax.nn.sigmoid(gate_bottom)

    @pl.when(i == pl.num_programs(0) - 1)
    def _():
        @pl.when(j == 0)
        def _():
            normalize(xs_ref, hs_ref)

        a, gate = project(hs_ref, slice(0, hs_ref.shape[0]))
        us_ref[...] = a * jax.nn.sigmoid(gate)


def _glu(x, xs, g, w_in, b_in, layer, norm_layer):
    m, ms = x.shape[0], xs.shape[0]
    tm = min(ROW_TILE, m)
    ni = m // tm
    nj = D_MODEL // COL_TILE

    def gate_tile(i, j):
        return (layer, 0, nj + jnp.where(i == 0, j, nj - 1))

    return pl.pallas_call(
        _glu_kernel,
        grid=(ni, nj),
        in_specs=[
            pl.BlockSpec((tm, D_MODEL), lambda i, j: (i, 0), pipeline_mode=pl.Buffered(1)),
            pl.BlockSpec((ms, D_MODEL), lambda i, j: (0, 0)),
            pl.BlockSpec((None, 1, D_MODEL), lambda i, j: (norm_layer, 0, 0)),
            pl.BlockSpec((None, D_MODEL, COL_TILE), _first_pass_tile(layer, nj)),
            pl.BlockSpec((None, D_MODEL, COL_TILE), gate_tile),
            pl.BlockSpec((None, 1, COL_TILE), lambda i, j: (layer, 0, j)),
            pl.BlockSpec((None, 1, COL_TILE), lambda i, j: (layer, 0, j + nj)),
        ],
        out_specs=[
            pl.BlockSpec((tm, COL_TILE), lambda i, j: (i, j)),
            pl.BlockSpec((ms, COL_TILE), _last_tile_only(ni, lambda j: j)),
        ],
        out_shape=[
            jax.ShapeDtypeStruct((m, D_MODEL), F32),
            jax.ShapeDtypeStruct((ms, D_MODEL), F32),
        ],
        scratch_shapes=[pltpu.VMEM((tm, D_MODEL), BF16),
                        pltpu.VMEM((ms, D_MODEL), BF16),
                        pltpu.VMEM((2 * nj, D_MODEL, COL_TILE), BF16)],
        compiler_params=_params("arbitrary", "arbitrary"),
        name="glu",
    )(x, xs, g, w_in, w_in, b_in, b_in)


def _ln_swish(c, g, b):
    mu = jnp.mean(c, axis=-1, keepdims=True)
    xc = c - mu
    var = jnp.mean(xc * xc, axis=-1, keepdims=True)
    y = xc * lax.rsqrt(var + LN_EPS) * g + b
    return y * jax.nn.sigmoid(y)


def _conv_lanes(win_ref, row0, w_ref, vec_ref, c_ref, c):
    off = CONV_HALO - (CONV_WIDTH - 1)
    n_win = CONV_HALO + CONV_ROWS
    cs = slice(c * LANES, (c + 1) * LANES)
    win = win_ref[row0:row0 + n_win, cs]
    acc = jnp.zeros((CONV_ROWS, LANES), F32)
    for phase in range(SUBLANES):
        taps = [j for j in range(CONV_WIDTH) if (off + j) % SUBLANES == phase]
        if not taps:
            continue
        shifted = win if phase == 0 else pltpu.roll(win, n_win - phase, 0)
        for j in taps:
            base = (off + j) - phase
            acc = acc + shifted[base:base + CONV_ROWS] * w_ref[j:j + 1, cs]
    c_ref[:, cs] = acc + vec_ref[0:1, cs]


def _conv_proj_kernel(cur_ref, halo_ref, wdw_ref, vec_ref, w_ref, res_ref,
                      as_ref, ress_ref, o_ref, os_ref, win_ref, conv_ref, c_ref, wbf_ref, *,
                      tiles_per_seq):
    s = pl.program_id(0)
    n_tiles = pl.num_programs(0) - 1

    @pl.when(s == 0)
    def _():
        for c in range(D_MODEL // COL_TILE):
            cs = slice(c * COL_TILE, (c + 1) * COL_TILE)
            wbf_ref[:, cs] = w_ref[:, cs].astype(BF16)
        c_ref[...] = jnp.zeros(c_ref.shape, BF16)

    tile = jnp.minimum(s, n_tiles - 1)
    halo = halo_ref[...]
    win_ref[0:CONV_HALO, :] = jnp.where(tile % tiles_per_seq > 0, halo, jnp.zeros_like(halo))
    win_ref[CONV_HALO:, :] = cur_ref[...]

    def project(n):
        ns = slice(n * PROJ_SLAB, (n + 1) * PROJ_SLAB)
        y = jnp.dot(c_ref[(s + 1) % 2], wbf_ref[:, ns], preferred_element_type=F32)
        o_ref[:, ns] = res_ref[:, ns] + y + vec_ref[3:4, ns]

    lane_groups = D_MODEL // LANES
    row_blocks = CONV_TILE_ROWS // CONV_ROWS
    n_slabs = D_MODEL // PROJ_SLAB
    every = (row_blocks * lane_groups) // n_slabs
    slot = s % 2
    for k in range(row_blocks):
        for c in range(lane_groups):
            idx = k * lane_groups + c
            if idx % every == 0:
                project(idx // every)
            _conv_lanes(win_ref, k * CONV_ROWS, wdw_ref, vec_ref, conv_ref, c)
        c_ref[slot, k * CONV_ROWS:(k + 1) * CONV_ROWS, :] = _ln_swish(
            conv_ref[...], vec_ref[1:2, :], vec_ref[2:3, :]).astype(BF16)

    @pl.when(s == n_tiles)
    def _():
        ys = jnp.dot(as_ref[...], wbf_ref[...], preferred_element_type=F32)
        os_ref[...] = ress_ref[...] + ys + vec_ref[3:4, :]


def _conv_proj(u, a_s, res, res_s, w_dw, conv_vecs, w_out, seq, layer):
    m = u.shape[0]
    ms = a_s.shape[0]
    n_tiles = m // CONV_TILE_ROWS
    ratio = CONV_TILE_ROWS // CONV_HALO
    const = lambda shape: pl.BlockSpec(shape, lambda s: (0, 0))

    def cur_tile(s):
        return jnp.minimum(s, n_tiles - 1)

    def prev_tile(s):
        return jnp.maximum(s - 1, 0)

    return pl.pallas_call(
        functools.partial(_conv_proj_kernel, tiles_per_seq=seq // CONV_TILE_ROWS),
        grid=(n_tiles + 1,),
        in_specs=[
            pl.BlockSpec((CONV_TILE_ROWS, D_MODEL), lambda s: (cur_tile(s), 0)),
            pl.BlockSpec((CONV_HALO, D_MODEL), lambda s: (jnp.maximum(cur_tile(s) * ratio - 1, 0), 0)),
            pl.BlockSpec((None, CONV_WIDTH, D_MODEL), lambda s: (layer, 0, 0)),
            pl.BlockSpec((None, 4, D_MODEL), lambda s: (layer, 0, 0)),
            pl.BlockSpec((None, D_MODEL, D_MODEL), lambda s: (layer, 0, 0)),
            pl.BlockSpec((CONV_TILE_ROWS, D_MODEL), lambda s: (prev_tile(s), 0)),
            const((ms, D_MODEL)),
            const((ms, D_MODEL)),
        ],
        out_specs=[
            pl.BlockSpec((CONV_TILE_ROWS, D_MODEL), lambda s: (prev_tile(s), 0)),
            const((ms, D_MODEL)),
        ],
        out_shape=[
            jax.ShapeDtypeStruct((m, D_MODEL), F32),
            jax.ShapeDtypeStruct((ms, D_MODEL), F32),
        ],
        scratch_shapes=[pltpu.VMEM((CONV_HALO + CONV_TILE_ROWS, D_MODEL), F32),
                        pltpu.VMEM((CONV_ROWS, D_MODEL), F32),
                        pltpu.VMEM((2, CONV_TILE_ROWS, D_MODEL), BF16),
                        pltpu.VMEM((D_MODEL, D_MODEL), BF16)],
        compiler_params=_params("arbitrary"),
        name="conv_proj",
    )(u, u, w_dw, conv_vecs, w_out, res, a_s, res_s)


CONV_SAMPLE_BATCH_BLOCK = 4
CONV_NEW_ROW0 = 32


def _conv_sample_kernel(state_ref, u_ref, w_ref, vec_ref, o_ref, win_ref, *, t_new):
    keep = CONV_WIDTH - 1
    gap = CONV_NEW_ROW0 - keep
    row = lax.broadcasted_iota(jnp.int32, (t_new, D_MODEL), 0)
    outs = []
    for bb in range(CONV_SAMPLE_BATCH_BLOCK):
        win_ref[bb, CONV_NEW_ROW0 - SUBLANES:CONV_NEW_ROW0, :] = jnp.zeros((SUBLANES, D_MODEL), F32)
        win_ref[bb, 0:keep, :] = state_ref[bb]
        win_ref[bb, CONV_NEW_ROW0:CONV_NEW_ROW0 + t_new, :] = u_ref[bb * t_new:(bb + 1) * t_new, :]
        acc = jnp.zeros((t_new, D_MODEL), F32)
        for j in range(CONV_WIDTH):
            if j + t_new <= keep:
                rows = win_ref[bb, j:j + t_new, :]
            else:
                old = win_ref[bb, j:j + t_new, :]
                fresh = win_ref[bb, j + gap:j + gap + t_new, :]
                rows = jnp.where(row + j < keep, old, fresh)
            acc = acc + rows * w_ref[j:j + 1, :]
        outs.append(acc + vec_ref[0:1, :])
    c = jnp.concatenate(outs, axis=0)
    o_ref[...] = _ln_swish(c, vec_ref[1:2, :], vec_ref[2:3, :]).astype(BF16)


def _conv_sample(state, u_s, w_dw, conv_vecs, layer):
    batch = state.shape[1]
    t_new = u_s.shape[0] // batch
    keep = CONV_WIDTH - 1
    return pl.pallas_call(
        functools.partial(_conv_sample_kernel, t_new=t_new),
        grid=(batch // CONV_SAMPLE_BATCH_BLOCK,),
        in_specs=[
            pl.BlockSpec((None, CONV_SAMPLE_BATCH_BLOCK, keep, D_MODEL), lambda b: (layer, b, 0, 0)),
            pl.BlockSpec((CONV_SAMPLE_BATCH_BLOCK * t_new, D_MODEL), lambda b: (b, 0)),
            pl.BlockSpec((None, CONV_WIDTH, D_MODEL), lambda b: (layer, 0, 0)),
            pl.BlockSpec((None, 4, D_MODEL), lambda b: (layer, 0, 0)),
        ],
        out_specs=pl.BlockSpec((CONV_SAMPLE_BATCH_BLOCK * t_new, D_MODEL), lambda b: (b, 0)),
        out_shape=jax.ShapeDtypeStruct((batch * t_new, D_MODEL), BF16),
        scratch_shapes=[pltpu.VMEM((CONV_SAMPLE_BATCH_BLOCK, CONV_NEW_ROW0 + t_new, D_MODEL), F32)],
        compiler_params=_params("parallel"),
        name="conv_sample",
    )(state, u_s, w_dw, conv_vecs)


def _rope_tables(pos):
    inv = ROPE_THETA ** (-jnp.arange(0, HEAD_DIM, 2, dtype=F32) / HEAD_DIM)
    ang = pos.astype(F32)[:, None] * inv[None, :]
    cos, sin = jnp.cos(ang), jnp.sin(ang)
    reps = LANES // HEAD_DIM
    cos_t = jnp.tile(jnp.concatenate([cos, cos], axis=-1), (1, reps))
    sin_t = jnp.tile(jnp.concatenate([-sin, sin], axis=-1), (1, reps))
    scale = HEAD_DIM ** -0.5
    cos_all = jnp.concatenate([cos_t * scale, cos_t, jnp.ones_like(cos_t)], axis=1)
    sin_all = jnp.concatenate([sin_t * scale, sin_t, jnp.zeros_like(sin_t)], axis=1)
    return cos_all, sin_all


def kernel(x_prompt, x_sample, cache_k, cache_v, state_conv, norm_mix, norm_mlp, norm_final,
           attn_w_qkv, attn_b_qkv, attn_sinks, attn_w_o, attn_b_o,
           conv_w_in, conv_b_in, conv_w_dw, conv_b_dw, conv_ln_g, conv_ln_b,
           conv_w_out, conv_b_out, mlp_w_up, mlp_w_down):
    batch, seq, _ = x_prompt.shape
    dbatch, t_new, _ = x_sample.shape
    xp = x_prompt.reshape(batch * seq, D_MODEL)
    xs = x_sample.reshape(dbatch * t_new, D_MODEL)

    cos_p, sin_p = _rope_tables(jnp.arange(seq, dtype=jnp.int32))
    cos_s, sin_s = _rope_tables(PAST_LEN + jnp.arange(t_new, dtype=jnp.int32))
    cos_s = jnp.tile(cos_s, (dbatch, 1))
    sin_s = jnp.tile(sin_s, (dbatch, 1))

    norm_mix, norm_mlp = _layer_vec(norm_mix), _layer_vec(norm_mlp)
    attn_b_qkv, attn_b_o = _layer_vec(attn_b_qkv), _layer_vec(attn_b_o)
    conv_b_in = _layer_vec(conv_b_in)
    conv_vecs = jnp.stack([conv_b_dw, conv_ln_g, conv_ln_b, conv_b_out], axis=1)
    n_attn = cache_k.shape[0]
    ck = cache_k.reshape(n_attn, dbatch, WINDOW, KV_DIM)
    cv = cache_v.reshape(n_attn, dbatch, WINDOW, KV_DIM)
    sink_rows = jnp.repeat(attn_sinks, t_new, axis=1)[:, :, None]
    g_final = norm_final.reshape(1, D_MODEL)

    k_p, v_p, c_p, k_new, v_new, u_new = [], [], [], [], [], []
    for i in range(DEPTH):
        j = i // 2
        if i % 2 == 0:
            q, kv, q_s, kv_s = _qkv_rope(xp, xs, norm_mix, attn_w_qkv, attn_b_qkv,
                                         (cos_p, sin_p), (cos_s, sin_s), j, i)
            o = _attn_prompt(q, kv, attn_sinks, batch, seq, j)
            o_s = _attn_sample(q_s, kv_s, ck, cv, sink_rows, dbatch, t_new, j)
            xp, xs = _proj_res(o, o_s, attn_w_o, attn_b_o, xp, xs, j)
            kv3 = kv.reshape(batch, seq, 2 * KV_DIM)[:, seq - WINDOW:]
            k_p.append(kv3[..., :KV_DIM].reshape(batch, WINDOW, N_KV_HEADS, HEAD_DIM))
            v_p.append(kv3[..., KV_DIM:].reshape(batch, WINDOW, N_KV_HEADS, HEAD_DIM))
            kv3 = kv_s.reshape(dbatch, t_new, 2 * KV_DIM)
            k_new.append(kv3[..., :KV_DIM].reshape(dbatch, t_new, N_KV_HEADS, HEAD_DIM))
            v_new.append(kv3[..., KV_DIM:].reshape(dbatch, t_new, N_KV_HEADS, HEAD_DIM))
        else:
            u, u_s = _glu(xp, xs, norm_mix, conv_w_in, conv_b_in, j, i)
            c_smp = _conv_sample(state_conv, u_s, conv_w_dw, conv_vecs, j)
            xp, xs = _conv_proj(u, c_smp, xp, xs, conv_w_dw, conv_vecs, conv_w_out, seq, j)
            c_p.append(u.reshape(batch, seq, D_MODEL)[:, seq - (CONV_WIDTH - 1):])
            u_new.append(u_s.reshape(dbatch, t_new, D_MODEL))
        xp, xs = _mlp(xp, xs, norm_mlp, mlp_w_up, mlp_w_down, g_final, i, i == DEPTH - 1)

    k_s = jnp.concatenate([cache_k[:, :, t_new:], jnp.stack(k_new)], axis=2)
    v_s = jnp.concatenate([cache_v[:, :, t_new:], jnp.stack(v_new)], axis=2)
    c_s = jnp.concatenate([state_conv[:, :, t_new:], jnp.stack(u_new)], axis=2)

    y_p = xp.reshape(batch, seq, D_MODEL)
    y_s = xs.reshape(dbatch, t_new, D_MODEL)
    return (y_p, y_s, jnp.stack(k_p), jnp.stack(v_p), jnp.stack(c_p), k_s, v_s, c_s)
```

```python
import functools

import jax
import jax.numpy as jnp
from jax import lax
from jax.experimental import pallas as pl
from jax.experimental.pallas import tpu as pltpu

D_MODEL = 2048
HEAD_DIM = 64
N_HEADS = 32
N_KV_HEADS = 8
GROUP = N_HEADS // N_KV_HEADS
Q_DIM = N_HEADS * HEAD_DIM
KV_DIM = N_KV_HEADS * HEAD_DIM
WINDOW = 128
PAST_LEN = 16384
ROPE_THETA = 10000.0
CONV_WIDTH = 31
D_FF = 4 * D_MODEL
RMS_EPS = 1e-6
LN_EPS = 1e-5
DEPTH = 4

VMEM_LIMIT_BYTES = 60 * 1024 * 1024
MLP_VMEM_LIMIT_BYTES = 62 * 1024 * 1024
LANES = 128
SUBLANES = 8
ROW_TILE = 1024
COL_TILE = 512
FF_TILE = 512
CONV_ROWS = 128
CONV_TILE_ROWS = 256
PROJ_SLAB = 256
CONV_HALO = 32
HEAD_SLAB = GROUP * HEAD_DIM

assert COL_TILE == KV_DIM and LANES == 2 * HEAD_DIM and HEAD_SLAB == 2 * LANES

BF16 = jnp.bfloat16
F32 = jnp.float32
NT_DIMS = (((1,), (1,)), ((), ()))


def _params(*sem):
    return pltpu.CompilerParams(dimension_semantics=sem, vmem_limit_bytes=VMEM_LIMIT_BYTES)


def _rms_rows(x, g):
    ms = jnp.mean(x * x, axis=-1, keepdims=True)
    return x * lax.rsqrt(ms + RMS_EPS) * g


def _layer_vec(v):
    return v.reshape(v.shape[0], 1, v.shape[1])


def _first_pass_tile(layer, n_tiles):
    def index_map(i, j):
        return (layer, 0, jnp.where(i == 0, j, n_tiles - 1))
    return index_map


def _last_tile_only(n_row_tiles, col_map):
    def index_map(i, j):
        return (0, jnp.where(i == n_row_tiles - 1, col_map(j), col_map(0)))
    return index_map


def _qkv_kernel(x_ref, xs_ref, g_ref, w_ref, b_ref, cos_ref, sin_ref, coss_ref, sins_ref,
                q_ref, kv_ref, qs_ref, kvs_ref, h_ref, hs_ref, wbf_ref):
    i = pl.program_id(0)
    j = pl.program_id(1)
    nq = Q_DIM // COL_TILE

    @pl.when(i == 0)
    def _():
        wbf_ref[j] = w_ref[...].astype(BF16)

    w = wbf_ref[j]
    bias = b_ref[...]
    reps = COL_TILE // LANES

    def normalize(x_r, h_r):
        h_r[...] = _rms_rows(x_r[...], g_ref[...]).astype(BF16)

    def project(h_r, rows):
        return jnp.dot(h_r[rows, :], w, preferred_element_type=F32) + bias

    def rotate_store(rows, y, cos_r, sin_r, q_r, kv_r):
        lane = lax.broadcasted_iota(jnp.int32, y.shape, 1)
        first_half = (lane % HEAD_DIM) < (HEAD_DIM // 2)
        rot = jnp.where(first_half,
                        pltpu.roll(y, COL_TILE - HEAD_DIM // 2, 1),
                        pltpu.roll(y, HEAD_DIM // 2, 1))
        r = (y * jnp.tile(cos_r[rows, :], (1, reps))
             + rot * jnp.tile(sin_r[rows, :], (1, reps)))

        @pl.when(j < nq)
        def _():
            q_r[rows, :] = r.astype(BF16)

        @pl.when(j >= nq)
        def _():
            kv_r[rows, :] = r

    @pl.when(j == 0)
    def _():
        normalize(x_ref, h_ref)

    half = h_ref.shape[0] // 2
    top, bottom = slice(0, half), slice(half, 2 * half)
    y_top = project(h_ref, top)
    y_bottom = project(h_ref, bottom)
    rotate_store(top, y_top, cos_ref, sin_ref, q_ref, kv_ref)
    rotate_store(bottom, y_bottom, cos_ref, sin_ref, q_ref, kv_ref)

    @pl.when(i == pl.num_programs(0) - 1)
    def _():
        @pl.when(j == 0)
        def _():
            normalize(xs_ref, hs_ref)

        rows = slice(0, hs_ref.shape[0])
        rotate_store(rows, project(hs_ref, rows), coss_ref, sins_ref, qs_ref, kvs_ref)


def _qkv_rope(x, xs, g, w, b, tabs, tabs_s, layer, norm_layer):
    m, ms = x.shape[0], xs.shape[0]
    tm = min(ROW_TILE, m)
    ni = m // tm
    cos_t, sin_t = tabs
    cos_s, sin_s = tabs_s
    n_tab = cos_t.shape[0] // tm
    nq = Q_DIM // COL_TILE
    nj = (Q_DIM + 2 * KV_DIM) // COL_TILE

    def tab_col(j):
        return jnp.maximum(j - nq + 1, 0)

    def q_col(j):
        return jnp.minimum(j, nq - 1)

    def kv_col(j):
        return jnp.maximum(j - nq, 0)

    def tab_map(i, j):
        return (i % n_tab, tab_col(j))

    return pl.pallas_call(
        _qkv_kernel,
        grid=(ni, nj),
        in_specs=[
            pl.BlockSpec((tm, D_MODEL), lambda i, j: (i, 0)),
            pl.BlockSpec((ms, D_MODEL), lambda i, j: (0, 0)),
            pl.BlockSpec((None, 1, D_MODEL), lambda i, j: (norm_layer, 0, 0)),
            pl.BlockSpec((None, D_MODEL, COL_TILE), _first_pass_tile(layer, nj)),
            pl.BlockSpec((None, 1, COL_TILE), lambda i, j: (layer, 0, j)),
            pl.BlockSpec((tm, LANES), tab_map),
            pl.BlockSpec((tm, LANES), tab_map),
            pl.BlockSpec((ms, LANES), _last_tile_only(ni, tab_col)),
            pl.BlockSpec((ms, LANES), _last_tile_only(ni, tab_col)),
        ],
        out_specs=[
            pl.BlockSpec((tm, COL_TILE), lambda i, j: (i, q_col(j))),
            pl.BlockSpec((tm, COL_TILE), lambda i, j: (i, kv_col(j))),
            pl.BlockSpec((ms, COL_TILE), _last_tile_only(ni, q_col)),
            pl.BlockSpec((ms, COL_TILE), _last_tile_only(ni, kv_col)),
        ],
        out_shape=[
            jax.ShapeDtypeStruct((m, Q_DIM), BF16),
            jax.ShapeDtypeStruct((m, 2 * KV_DIM), F32),
            jax.ShapeDtypeStruct((ms, Q_DIM), BF16),
            jax.ShapeDtypeStruct((ms, 2 * KV_DIM), F32),
        ],
        scratch_shapes=[pltpu.VMEM((tm, D_MODEL), BF16),
                        pltpu.VMEM((ms, D_MODEL), BF16),
                        pltpu.VMEM((nj, D_MODEL, COL_TILE), BF16)],
        compiler_params=_params("arbitrary", "arbitrary"),
        name="qkv_rope",
    )(x, xs, g, w, b, cos_t, sin_t, cos_s, sin_s)


def _rep_heads(x):
    n = x.shape[1]
    lane = lax.broadcasted_iota(jnp.int32, x.shape, 1)
    low = (lane % LANES) < HEAD_DIM
    even = jnp.where(low, x, pltpu.roll(x, HEAD_DIM, 1))
    odd = jnp.where(low, pltpu.roll(x, n - HEAD_DIM, 1), x)
    pieces = []
    for s in range(n // LANES):
        sl = slice(s * LANES, (s + 1) * LANES)
        pieces += [even[:, sl], even[:, sl], odd[:, sl], odd[:, sl]]
    return jnp.concatenate(pieces, axis=1)


ATTN_SEQS = 4


def _attn_prompt_kernel(sinks_ref, q_ref, kc_ref, vc_ref, o_ref, kb_ref, vt_ref, ot_ref, *, layer):
    n = pl.program_id(1)

    @pl.when(n == 0)
    def _():
        kb_ref[:, 0:WINDOW, :] = jnp.zeros((ATTN_SEQS, WINDOW, Q_DIM), BF16)
        vt_ref[:, :, 0:WINDOW] = jnp.zeros((ATTN_SEQS, KV_DIM, WINDOW), BF16)

    @pl.when(n > 0)
    def _():
        kb_ref[:, 0:WINDOW, :] = kb_ref[:, WINDOW:, :]
        vt_ref[:, :, 0:WINDOW] = vt_ref[:, :, WINDOW:]

    for b in range(ATTN_SEQS):
        kb_ref[b, WINDOW:, :] = _rep_heads(kc_ref[b]).astype(BF16)
        vt_ref[b, :, WINDOW:] = vc_ref[b].T.astype(BF16)

    cols = GROUP * WINDOW
    key = lax.broadcasted_iota(jnp.int32, (2 * WINDOW, cols), 0)
    qi = lax.broadcasted_iota(jnp.int32, (2 * WINDOW, cols), 1) % WINDOW
    allowed = jnp.logical_or(
        jnp.logical_and(key < WINDOW, jnp.logical_and(key >= qi, n > 0)),
        jnp.logical_and(key >= WINDOW, key - WINDOW <= qi))
    bias = jnp.where(allowed, 0.0, -jnp.inf).astype(F32)
    lane_group = lax.broadcasted_iota(jnp.int32, (WINDOW, HEAD_SLAB), 1) // HEAD_DIM
    col_group = lax.broadcasted_iota(jnp.int32, (1, cols), 1) // WINDOW

    def scores(b, h):
        hs = slice(h * HEAD_SLAB, (h + 1) * HEAD_SLAB)
        qh = q_ref[b, :, hs]
        zero = jnp.zeros_like(qh)
        qm = jnp.concatenate([jnp.where(lane_group == g, qh, zero) for g in range(GROUP)], axis=0)
        return lax.dot_general(kb_ref[b, :, hs], qm, NT_DIMS, preferred_element_type=F32) + bias

    def sink_row(h):
        sink = jnp.full((1, cols), sinks_ref[layer, h * GROUP + GROUP - 1], F32)
        for g in range(GROUP - 2, -1, -1):
            sink = jnp.where(col_group == g, sinks_ref[layer, h * GROUP + g], sink)
        return sink

    def softmax(s, sink):
        mx = jnp.maximum(jnp.max(s, axis=0, keepdims=True), sink)
        p = jnp.exp(s - mx)
        denom = jnp.sum(p, axis=0, keepdims=True) + jnp.exp(sink - mx)
        return p.astype(BF16), 1.0 / denom

    def finish(b, h, p, inv):
        ot = jnp.dot(vt_ref[b, h * HEAD_DIM:(h + 1) * HEAD_DIM, :], p,
                     preferred_element_type=F32) * inv
        for g in range(GROUP):
            r0 = h * HEAD_SLAB + g * HEAD_DIM
            ot_ref[b, r0:r0 + HEAD_DIM, :] = ot[:, g * WINDOW:(g + 1) * WINDOW]

    work = [(b, h) for b in range(ATTN_SEQS) for h in range(N_KV_HEADS)]
    s_next = scores(*work[0])
    pending = None
    for idx, (b, h) in enumerate(work):
        s = s_next
        if idx + 1 < len(work):
            s_next = scores(*work[idx + 1])
        p, inv = softmax(s, sink_row(h))
        if pending is not None:
            finish(*pending)
        pending = (b, h, p, inv)
    finish(*pending)
    for b in range(ATTN_SEQS):
        o_ref[b] = ot_ref[b].T.astype(BF16)


def _attn_prompt(q, kv, sinks, batch, seq, layer):
    nb = seq // WINDOW
    q3 = q.reshape(batch, seq, Q_DIM)
    kv3 = kv.reshape(batch, seq, 2 * KV_DIM)
    out = pl.pallas_call(
        functools.partial(_attn_prompt_kernel, layer=layer),
        grid=(batch // ATTN_SEQS, nb),
        in_specs=[
            pl.BlockSpec(memory_space=pltpu.SMEM),
            pl.BlockSpec((ATTN_SEQS, WINDOW, Q_DIM), lambda b, n: (b, n, 0)),
            pl.BlockSpec((ATTN_SEQS, WINDOW, KV_DIM), lambda b, n: (b, n, 0)),
            pl.BlockSpec((ATTN_SEQS, WINDOW, KV_DIM), lambda b, n: (b, n, 1)),
        ],
        out_specs=pl.BlockSpec((ATTN_SEQS, WINDOW, Q_DIM), lambda b, n: (b, n, 0)),
        out_shape=jax.ShapeDtypeStruct((batch, seq, Q_DIM), BF16),
        scratch_shapes=[pltpu.VMEM((ATTN_SEQS, 2 * WINDOW, Q_DIM), BF16),
                        pltpu.VMEM((ATTN_SEQS, KV_DIM, 2 * WINDOW), BF16),
                        pltpu.VMEM((ATTN_SEQS, Q_DIM, WINDOW), F32)],
        compiler_params=_params("parallel", "arbitrary"),
        name="attn_prompt",
    )(sinks, q3, kv3, kv3)
    return out.reshape(batch * seq, Q_DIM)


SAMPLE_BATCH_BLOCK = 8


def _attn_sample_kernel(sink_ref, q_ref, kvn_ref, ck_ref, cv_ref, o_ref, own_ref, *, t_new):
    pad = 2 * SUBLANES - t_new
    n_keys = WINDOW + t_new + pad
    rows = N_HEADS * t_new
    t_of_row = lax.broadcasted_iota(jnp.int32, (rows, n_keys), 0) % t_new
    col = lax.broadcasted_iota(jnp.int32, (rows, n_keys), 1)
    allowed = jnp.logical_or(
        jnp.logical_and(col < WINDOW, col >= t_of_row),
        jnp.logical_and(col >= WINDOW, col - WINDOW <= t_of_row))
    bias = jnp.where(allowed, 0.0, -jnp.inf).astype(F32)
    head_of_row = lax.broadcasted_iota(jnp.int32, (rows, Q_DIM), 0) // t_new
    head_of_lane = lax.broadcasted_iota(jnp.int32, (rows, Q_DIM), 1) // HEAD_DIM
    own_ref[...] = jnp.where(head_of_row == head_of_lane, 1.0, 0.0).astype(F32)
    sink = sink_ref[...]
    q_all = q_ref[...].astype(F32)
    kvn = kvn_ref[...]
    zpad = jnp.zeros((pad, KV_DIM), F32)

    def scores(bb):
        r0 = bb * t_new
        k_new = jnp.concatenate([kvn[r0:r0 + t_new, :KV_DIM], zpad], axis=0)
        k_all = jnp.concatenate([_rep_heads(ck_ref[bb]).astype(BF16),
                                 _rep_heads(k_new).astype(BF16)], axis=0)
        qf = (jnp.tile(q_all[r0:r0 + t_new], (N_HEADS, 1)) * own_ref[...]).astype(BF16)
        return lax.dot_general(qf, k_all, NT_DIMS, preferred_element_type=F32) + bias

    def softmax(s):
        mx = jnp.maximum(jnp.max(s, axis=-1, keepdims=True), sink)
        p = jnp.exp(s - mx)
        denom = jnp.sum(p, axis=-1, keepdims=True) + jnp.exp(sink - mx)
        return (p * (1.0 / denom)).astype(BF16)

    def finish(bb, pn):
        r0 = bb * t_new
        v_new = jnp.concatenate([kvn[r0:r0 + t_new, KV_DIM:], zpad], axis=0)
        v_all = jnp.concatenate([_rep_heads(cv_ref[bb]).astype(BF16),
                                 _rep_heads(v_new).astype(BF16)], axis=0)
        pv = jnp.dot(pn, v_all, preferred_element_type=F32) * own_ref[...]
        o = pv[0:t_new]
        for h in range(1, N_HEADS):
            o = o + pv[h * t_new:(h + 1) * t_new]
        o_ref[r0:r0 + t_new, :] = o.astype(BF16)

    s_next = scores(0)
    pending = None
    for bb in range(SAMPLE_BATCH_BLOCK):
        s = s_next
        if bb + 1 < SAMPLE_BATCH_BLOCK:
            s_next = scores(bb + 1)
        pn = softmax(s)
        if pending is not None:
            finish(*pending)
        pending = (bb, pn)
    finish(*pending)


def _attn_sample(q, kv, cache_k, cache_v, sink_rows, batch, t_new, layer):
    rows = SAMPLE_BATCH_BLOCK * t_new
    cache_spec = pl.BlockSpec((None, SAMPLE_BATCH_BLOCK, WINDOW, KV_DIM), lambda b: (layer, b, 0, 0))
    return pl.pallas_call(
        functools.partial(_attn_sample_kernel, t_new=t_new),
        grid=(batch // SAMPLE_BATCH_BLOCK,),
        in_specs=[
            pl.BlockSpec((None, N_HEADS * t_new, 1), lambda b: (layer, 0, 0)),
            pl.BlockSpec((rows, Q_DIM), lambda b: (b, 0)),
            pl.BlockSpec((rows, 2 * KV_DIM), lambda b: (b, 0)),
            cache_spec,
            cache_spec,
        ],
        out_specs=pl.BlockSpec((rows, Q_DIM), lambda b: (b, 0)),
        out_shape=jax.ShapeDtypeStruct((batch * t_new, Q_DIM), BF16),
        scratch_shapes=[pltpu.VMEM((N_HEADS * t_new, Q_DIM), F32)],
        compiler_params=_params("parallel"),
        name="attn_sample",
    )(sink_rows, q, kv, cache_k, cache_v)


def _proj_res_kernel(a_ref, as_ref, w_ref, b_ref, res_ref, ress_ref, o_ref, os_ref, wbf_ref):
    i = pl.program_id(0)
    j = pl.program_id(1)

    @pl.when(i == 0)
    def _():
        wbf_ref[j] = w_ref[...].astype(BF16)

    def project(a_r, res_r, o_r):
        y = jnp.dot(a_r[...], wbf_ref[j], preferred_element_type=F32)
        o_r[...] = res_r[...] + y + b_ref[...]

    project(a_ref, res_ref, o_ref)

    @pl.when(i == pl.num_programs(0) - 1)
    def _():
        project(as_ref, ress_ref, os_ref)


def _proj_res(a, a_s, w, b, res, res_s, layer):
    m, k = a.shape
    ms = a_s.shape[0]
    n = w.shape[-1]
    tm = min(ROW_TILE, m)
    ni = m // tm
    nj = n // COL_TILE
    sample_tile = _last_tile_only(ni, lambda j: j)
    return pl.pallas_call(
        _proj_res_kernel,
        grid=(ni, nj),
        in_specs=[
            pl.BlockSpec((tm, k), lambda i, j: (i, 0)),
            pl.BlockSpec((ms, k), lambda i, j: (0, 0)),
            pl.BlockSpec((None, k, COL_TILE), _first_pass_tile(layer, nj)),
            pl.BlockSpec((None, 1, COL_TILE), lambda i, j: (layer, 0, j)),
            pl.BlockSpec((tm, COL_TILE), lambda i, j: (i, j)),
            pl.BlockSpec((ms, COL_TILE), sample_tile),
        ],
        out_specs=[
            pl.BlockSpec((tm, COL_TILE), lambda i, j: (i, j)),
            pl.BlockSpec((ms, COL_TILE), sample_tile),
        ],
        out_shape=[
            jax.ShapeDtypeStruct((m, n), F32),
            jax.ShapeDtypeStruct((ms, n), F32),
        ],
        scratch_shapes=[pltpu.VMEM((nj, k, COL_TILE), BF16)],
        compiler_params=_params("arbitrary", "arbitrary"),
        name="proj_res",
    )(a, a_s, w, b, res, res_s)


def _mlp_kernel(x_ref, xs_ref, g_ref, wu_ref, wd_ref, gf_ref, o_ref, os_ref, h_ref, hs_ref, *,
                final_norm):
    i = pl.program_id(0)
    f = pl.program_id(1)
    last_i = pl.num_programs(0) - 1
    last_f = pl.num_programs(1) - 1

    def start(x_r, h_r, o_r):
        x = x_r[...]
        h_r[...] = _rms_rows(x, g_ref[...]).astype(BF16)
        o_r[...] = x

    def accumulate(h_r, o_r):
        a = jnp.dot(h_r[...], wu_ref[...].astype(BF16), preferred_element_type=F32)
        a = jnp.maximum(a, 0.0)
        a = (a * a).astype(BF16)
        o_r[...] += jnp.dot(a, wd_ref[...].astype(BF16), preferred_element_type=F32)

    def finish(o_r):
        o_r[...] = _rms_rows(o_r[...], gf_ref[...])

    @pl.when(f == 0)
    def _():
        start(x_ref, h_ref, o_ref)

    accumulate(h_ref, o_ref)

    if final_norm:
        @pl.when(f == last_f)
        def _():
            finish(o_ref)

    @pl.when(i == last_i)
    def _():
        @pl.when(f == 0)
        def _():
            start(xs_ref, hs_ref, os_ref)

        accumulate(hs_ref, os_ref)

        if final_norm:
            @pl.when(f == last_f)
            def _():
                finish(os_ref)


def _mlp(x, xs, g, w_up, w_down, g_final, layer, final_norm):
    m = x.shape[0]
    ms = xs.shape[0]
    tm = min(ROW_TILE, m)
    return pl.pallas_call(
        functools.partial(_mlp_kernel, final_norm=final_norm),
        grid=(m // tm, D_FF // FF_TILE),
        in_specs=[
            pl.BlockSpec((tm, D_MODEL), lambda i, f: (i, 0)),
            pl.BlockSpec((ms, D_MODEL), lambda i, f: (0, 0)),
            pl.BlockSpec((None, 1, D_MODEL), lambda i, f: (layer, 0, 0)),
            pl.BlockSpec((None, D_MODEL, FF_TILE), lambda i, f: (layer, 0, f)),
            pl.BlockSpec((None, FF_TILE, D_MODEL), lambda i, f: (layer, f, 0)),
            pl.BlockSpec((1, D_MODEL), lambda i, f: (0, 0)),
        ],
        out_specs=[
            pl.BlockSpec((tm, D_MODEL), lambda i, f: (i, 0)),
            pl.BlockSpec((ms, D_MODEL), lambda i, f: (0, 0)),
        ],
        out_shape=[
            jax.ShapeDtypeStruct((m, D_MODEL), F32),
            jax.ShapeDtypeStruct((ms, D_MODEL), F32),
        ],
        scratch_shapes=[pltpu.VMEM((tm, D_MODEL), BF16), pltpu.VMEM((ms, D_MODEL), BF16)],
        compiler_params=pltpu.CompilerParams(dimension_semantics=("arbitrary", "arbitrary"),
                                             vmem_limit_bytes=MLP_VMEM_LIMIT_BYTES),
        name="mlp",
    )(x, xs, g, w_up, w_down, g_final)


def _glu_kernel(x_ref, g_ref, wa_ref, wg_ref, ba_ref, bg_ref, u_ref, h_ref, wbf_ref):
    j = pl.program_id(1)
    nj = D_MODEL // COL_TILE

    @pl.when(pl.program_id(0) == 0)
    def _():
        wbf_ref[j] = wa_ref[...].astype(BF16)
        wbf_ref[nj + j] = wg_ref[...].astype(BF16)

    @pl.when(j == 0)
    def _():
        h_ref[...] = _rms_rows(x_ref[...], g_ref[...]).astype(BF16)

    wa, wg = wbf_ref[j], wbf_ref[nj + j]

    def project(rows):
        h = h_ref[rows, :]
        return (jnp.dot(h, wa, preferred_element_type=F32) + ba_ref[...],
                jnp.dot(h, wg, preferred_element_type=F32) + bg_ref[...])

    half = h_ref.shape[0] // 2
    top, bottom = slice(0, half), slice(half, 2 * half)
    a_top, gate_top = project(top)
    a_bottom, gate_bottom = project(bottom)
    u_ref[top, :] = a_top * jax.nn.sigmoid(gate_top)
    u_ref[bottom, :] = a_bottom * jax.nn.sigmoid(gate_bottom)


def _glu(x, g, w_in, b_in, layer, norm_layer):
    m = x.shape[0]
    tm = min(ROW_TILE, m)
    nj = D_MODEL // COL_TILE

    def gate_tile(i, j):
        return (layer, 0, nj + jnp.where(i == 0, j, nj - 1))

    return pl.pallas_call(
        _glu_kernel,
        grid=(m // tm, nj),
        in_specs=[
            pl.BlockSpec((tm, D_MODEL), lambda i, j: (i, 0)),
            pl.BlockSpec((None, 1, D_MODEL), lambda i, j: (norm_layer, 0, 0)),
            pl.BlockSpec((None, D_MODEL, COL_TILE), _first_pass_tile(layer, nj)),
            pl.BlockSpec((None, D_MODEL, COL_TILE), gate_tile),
            pl.BlockSpec((None, 1, COL_TILE), lambda i, j: (layer, 0, j)),
            pl.BlockSpec((None, 1, COL_TILE), lambda i, j: (layer, 0, j + nj)),
        ],
        out_specs=pl.BlockSpec((tm, COL_TILE), lambda i, j: (i, j)),
        out_shape=jax.ShapeDtypeStruct((m, D_MODEL), F32),
        scratch_shapes=[pltpu.VMEM((tm, D_MODEL), BF16),
                        pltpu.VMEM((2 * nj, D_MODEL, COL_TILE), BF16)],
        compiler_params=_params("arbitrary", "arbitrary"),
        name="glu",
    )(x, g, w_in, w_in, b_in, b_in)


def _ln_swish(c, g, b):
    mu = jnp.mean(c, axis=-1, keepdims=True)
    xc = c - mu
    var = jnp.mean(xc * xc, axis=-1, keepdims=True)
    y = xc * lax.rsqrt(var + LN_EPS) * g + b
    return y * jax.nn.sigmoid(y)


def _conv_lanes(win_ref, row0, w_ref, vec_ref, c_ref, c):
    off = CONV_HALO - (CONV_WIDTH - 1)
    n_win = CONV_HALO + CONV_ROWS
    cs = slice(c * LANES, (c + 1) * LANES)
    win = win_ref[row0:row0 + n_win, cs]
    acc = jnp.zeros((CONV_ROWS, LANES), F32)
    for phase in range(SUBLANES):
        taps = [j for j in range(CONV_WIDTH) if (off + j) % SUBLANES == phase]
        if not taps:
            continue
        shifted = win if phase == 0 else pltpu.roll(win, n_win - phase, 0)
        for j in taps:
            base = (off + j) - phase
            acc = acc + shifted[base:base + CONV_ROWS] * w_ref[j:j + 1, cs]
    c_ref[:, cs] = acc + vec_ref[0:1, cs]


def _conv_proj_kernel(cur_ref, halo_ref, wdw_ref, vec_ref, w_ref, res_ref,
                      as_ref, ress_ref, o_ref, os_ref, win_ref, conv_ref, c_ref, wbf_ref, *,
                      tiles_per_seq):
    s = pl.program_id(0)
    n_tiles = pl.num_programs(0) - 1

    @pl.when(s == 0)
    def _():
        for c in range(D_MODEL // COL_TILE):
            cs = slice(c * COL_TILE, (c + 1) * COL_TILE)
            wbf_ref[:, cs] = w_ref[:, cs].astype(BF16)
        c_ref[...] = jnp.zeros(c_ref.shape, BF16)

    tile = jnp.minimum(s, n_tiles - 1)
    halo = halo_ref[...]
    win_ref[0:CONV_HALO, :] = jnp.where(tile % tiles_per_seq > 0, halo, jnp.zeros_like(halo))
    win_ref[CONV_HALO:, :] = cur_ref[...]

    def project(n):
        ns = slice(n * PROJ_SLAB, (n + 1) * PROJ_SLAB)
        y = jnp.dot(c_ref[(s + 1) % 2], wbf_ref[:, ns], preferred_element_type=F32)
        o_ref[:, ns] = res_ref[:, ns] + y + vec_ref[3:4, ns]

    lane_groups = D_MODEL // LANES
    row_blocks = CONV_TILE_ROWS // CONV_ROWS
    n_slabs = D_MODEL // PROJ_SLAB
    every = (row_blocks * lane_groups) // n_slabs
    slot = s % 2
    for k in range(row_blocks):
        for c in range(lane_groups):
            idx = k * lane_groups + c
            if idx % every == 0:
                project(idx // every)
            _conv_lanes(win_ref, k * CONV_ROWS, wdw_ref, vec_ref, conv_ref, c)
        c_ref[slot, k * CONV_ROWS:(k + 1) * CONV_ROWS, :] = _ln_swish(
            conv_ref[...], vec_ref[1:2, :], vec_ref[2:3, :]).astype(BF16)

    @pl.when(s == n_tiles)
    def _():
        ys = jnp.dot(as_ref[...], wbf_ref[...], preferred_element_type=F32)
        os_ref[...] = ress_ref[...] + ys + vec_ref[3:4, :]


def _conv_proj(u, a_s, res, res_s, w_dw, conv_vecs, w_out, seq, layer):
    m = u.shape[0]
    ms = a_s.shape[0]
    n_tiles = m // CONV_TILE_ROWS
    ratio = CONV_TILE_ROWS // CONV_HALO
    const = lambda shape: pl.BlockSpec(shape, lambda s: (0, 0))

    def cur_tile(s):
        return jnp.minimum(s, n_tiles - 1)

    def prev_tile(s):
        return jnp.maximum(s - 1, 0)

    return pl.pallas_call(
        functools.partial(_conv_proj_kernel, tiles_per_seq=seq // CONV_TILE_ROWS),
        grid=(n_tiles + 1,),
        in_specs=[
            pl.BlockSpec((CONV_TILE_ROWS, D_MODEL), lambda s: (cur_tile(s), 0)),
            pl.BlockSpec((CONV_HALO, D_MODEL), lambda s: (jnp.maximum(cur_tile(s) * ratio - 1, 0), 0)),
            pl.BlockSpec((None, CONV_WIDTH, D_MODEL), lambda s: (layer, 0, 0)),
            pl.BlockSpec((None, 4, D_MODEL), lambda s: (layer, 0, 0)),
            pl.BlockSpec((None, D_MODEL, D_MODEL), lambda s: (layer, 0, 0)),
            pl.BlockSpec((CONV_TILE_ROWS, D_MODEL), lambda s: (prev_tile(s), 0)),
            const((ms, D_MODEL)),
            const((ms, D_MODEL)),
        ],
        out_specs=[
            pl.BlockSpec((CONV_TILE_ROWS, D_MODEL), lambda s: (prev_tile(s), 0)),
            const((ms, D_MODEL)),
        ],
        out_shape=[
            jax.ShapeDtypeStruct((m, D_MODEL), F32),
            jax.ShapeDtypeStruct((ms, D_MODEL), F32),
        ],
        scratch_shapes=[pltpu.VMEM((CONV_HALO + CONV_TILE_ROWS, D_MODEL), F32),
                        pltpu.VMEM((CONV_ROWS, D_MODEL), F32),
                        pltpu.VMEM((2, CONV_TILE_ROWS, D_MODEL), BF16),
                        pltpu.VMEM((D_MODEL, D_MODEL), BF16)],
        compiler_params=_params("arbitrary"),
        name="conv_proj",
    )(u, u, w_dw, conv_vecs, w_out, res, a_s, res_s)


CONV_SAMPLE_BATCH_BLOCK = 4
CONV_NEW_ROW0 = 32


def _conv_sample_kernel(state_ref, u_ref, w_ref, vec_ref, o_ref, win_ref, *, t_new):
    keep = CONV_WIDTH - 1
    gap = CONV_NEW_ROW0 - keep
    row = lax.broadcasted_iota(jnp.int32, (t_new, D_MODEL), 0)
    outs = []
    for bb in range(CONV_SAMPLE_BATCH_BLOCK):
        win_ref[bb, CONV_NEW_ROW0 - SUBLANES:CONV_NEW_ROW0, :] = jnp.zeros((SUBLANES, D_MODEL), F32)
        win_ref[bb, 0:keep, :] = state_ref[bb]
        win_ref[bb, CONV_NEW_ROW0:CONV_NEW_ROW0 + t_new, :] = u_ref[bb * t_new:(bb + 1) * t_new, :]
        acc = jnp.zeros((t_new, D_MODEL), F32)
        for j in range(CONV_WIDTH):
            if j + t_new <= keep:
                rows = win_ref[bb, j:j + t_new, :]
            else:
                old = win_ref[bb, j:j + t_new, :]
                fresh = win_ref[bb, j + gap:j + gap + t_new, :]
                rows = jnp.where(row + j < keep, old, fresh)
            acc = acc + rows * w_ref[j:j + 1, :]
        outs.append(acc + vec_ref[0:1, :])
    c = jnp.concatenate(outs, axis=0)
    o_ref[...] = _ln_swish(c, vec_ref[1:2, :], vec_ref[2:3, :]).astype(BF16)


def _conv_sample(state, u_s, w_dw, conv_vecs, layer):
    batch = state.shape[1]
    t_new = u_s.shape[0] // batch
    keep = CONV_WIDTH - 1
    return pl.pallas_call(
        functools.partial(_conv_sample_kernel, t_new=t_new),
        grid=(batch // CONV_SAMPLE_BATCH_BLOCK,),
        in_specs=[
            pl.BlockSpec((None, CONV_SAMPLE_BATCH_BLOCK, keep, D_MODEL), lambda b: (layer, b, 0, 0)),
            pl.BlockSpec((CONV_SAMPLE_BATCH_BLOCK * t_new, D_MODEL), lambda b: (b, 0)),
            pl.BlockSpec((None, CONV_WIDTH, D_MODEL), lambda b: (layer, 0, 0)),
            pl.BlockSpec((None, 4, D_MODEL), lambda b: (layer, 0, 0)),
        ],
        out_specs=pl.BlockSpec((CONV_SAMPLE_BATCH_BLOCK * t_new, D_MODEL), lambda b: (b, 0)),
        out_shape=jax.ShapeDtypeStruct((batch * t_new, D_MODEL), BF16),
        scratch_shapes=[pltpu.VMEM((CONV_SAMPLE_BATCH_BLOCK, CONV_NEW_ROW0 + t_new, D_MODEL), F32)],
        compiler_params=_params("parallel"),
        name="conv_sample",
    )(state, u_s, w_dw, conv_vecs)


def _rope_tables(pos):
    inv = ROPE_THETA ** (-jnp.arange(0, HEAD_DIM, 2, dtype=F32) / HEAD_DIM)
    ang = pos.astype(F32)[:, None] * inv[None, :]
    cos, sin = jnp.cos(ang), jnp.sin(ang)
    reps = LANES // HEAD_DIM
    cos_t = jnp.tile(jnp.concatenate([cos, cos], axis=-1), (1, reps))
    sin_t = jnp.tile(jnp.concatenate([-sin, sin], axis=-1), (1, reps))
    scale = HEAD_DIM ** -0.5
    cos_all = jnp.concatenate([cos_t * scale, cos_t, jnp.ones_like(cos_t)], axis=1)
    sin_all = jnp.concatenate([sin_t * scale, sin_t, jnp.zeros_like(sin_t)], axis=1)
    return cos_all, sin_all


def kernel(x_prompt, x_sample, cache_k, cache_v, state_conv, norm_mix, norm_mlp, norm_final,
           attn_w_qkv, attn_b_qkv, attn_sinks, attn_w_o, attn_b_o,
           conv_w_in, conv_b_in, conv_w_dw, conv_b_dw, conv_ln_g, conv_ln_b,
           conv_w_out, conv_b_out, mlp_w_up, mlp_w_down):
    batch, seq, _ = x_prompt.shape
    dbatch, t_new, _ = x_sample.shape
    xp = x_prompt.reshape(batch * seq, D_MODEL)
    xs = x_sample.reshape(dbatch * t_new, D_MODEL)

    cos_p, sin_p = _rope_tables(jnp.arange(seq, dtype=jnp.int32))
    cos_s, sin_s = _rope_tables(PAST_LEN + jnp.arange(t_new, dtype=jnp.int32))
    cos_s = jnp.tile(cos_s, (dbatch, 1))
    sin_s = jnp.tile(sin_s, (dbatch, 1))

    norm_mix, norm_mlp = _layer_vec(norm_mix), _layer_vec(norm_mlp)
    attn_b_qkv, attn_b_o = _layer_vec(attn_b_qkv), _layer_vec(attn_b_o)
    conv_b_in = _layer_vec(conv_b_in)
    conv_vecs = jnp.stack([conv_b_dw, conv_ln_g, conv_ln_b, conv_b_out], axis=1)
    n_attn = cache_k.shape[0]
    ck = cache_k.reshape(n_attn, dbatch, WINDOW, KV_DIM)
    cv = cache_v.reshape(n_attn, dbatch, WINDOW, KV_DIM)
    sink_rows = jnp.repeat(attn_sinks, t_new, axis=1)[:, :, None]
    g_final = norm_final.reshape(1, D_MODEL)

    k_p, v_p, c_p, k_new, v_new, u_new = [], [], [], [], [], []
    for i in range(DEPTH):
        j = i // 2
        if i % 2 == 0:
            q, kv, q_s, kv_s = _qkv_rope(xp, xs, norm_mix, attn_w_qkv, attn_b_qkv,
                                         (cos_p, sin_p), (cos_s, sin_s), j, i)
            o = _attn_prompt(q, kv, attn_sinks, batch, seq, j)
            o_s = _attn_sample(q_s, kv_s, ck, cv, sink_rows, dbatch, t_new, j)
            xp, xs = _proj_res(o, o_s, attn_w_o, attn_b_o, xp, xs, j)
            kv3 = kv.reshape(batch, seq, 2 * KV_DIM)[:, seq - WINDOW:]
            k_p.append(kv3[..., :KV_DIM].reshape(batch, WINDOW, N_KV_HEADS, HEAD_DIM))
            v_p.append(kv3[..., KV_DIM:].reshape(batch, WINDOW, N_KV_HEADS, HEAD_DIM))
            kv3 = kv_s.reshape(dbatch, t_new, 2 * KV_DIM)
            k_new.append(kv3[..., :KV_DIM].reshape(dbatch, t_new, N_KV_HEADS, HEAD_DIM))
            v_new.append(kv3[..., KV_DIM:].reshape(dbatch, t_new, N_KV_HEADS, HEAD_DIM))
        else:
            u = _glu(xp, norm_mix, conv_w_in, conv_b_in, j, i)
            u_s = _glu(xs, norm_mix, conv_w_in, conv_b_in, j, i)
            c_smp = _conv_sample(state_conv, u_s, conv_w_dw, conv_vecs, j)
            xp, xs = _conv_proj(u, c_smp, xp, xs, conv_w_dw, conv_vecs, conv_w_out, seq, j)
            c_p.append(u.reshape(batch, seq, D_MODEL)[:, seq - (CONV_WIDTH - 1):])
            u_new.append(u_s.reshape(dbatch, t_new, D_MODEL))
        xp, xs = _mlp(xp, xs, norm_mlp, mlp_w_up, mlp_w_down, g_final, i, i == DEPTH - 1)

    k_s = jnp.concatenate([cache_k[:, :, t_new:], jnp.stack(k_new)], axis=2)
    v_s = jnp.concatenate([cache_v[:, :, t_new:], jnp.stack(v_new)], axis=2)
    c_s = jnp.concatenate([state_conv[:, :, t_new:], jnp.stack(u_new)], axis=2)

    y_p = xp.reshape(batch, seq, D_MODEL)
    y_s = xs.reshape(dbatch, t_new, D_MODEL)
    return (y_p, y_s, jnp.stack(k_p), jnp.stack(v_p), jnp.stack(c_p), k_s, v_s, c_s)
```

```python
import functools

import jax
import jax.numpy as jnp
from jax import lax
from jax.experimental import pallas as pl
from jax.experimental.pallas import tpu as pltpu

D_MODEL = 2048
HEAD_DIM = 64
N_HEADS = 32
N_KV_HEADS = 8
GROUP = N_HEADS // N_KV_HEADS
Q_DIM = N_HEADS * HEAD_DIM
KV_DIM = N_KV_HEADS * HEAD_DIM
WINDOW = 128
PAST_LEN = 16384
ROPE_THETA = 10000.0
CONV_WIDTH = 31
D_FF = 4 * D_MODEL
RMS_EPS = 1e-6
LN_EPS = 1e-5
DEPTH = 4

VMEM_LIMIT_BYTES = 60 * 1024 * 1024
MLP_VMEM_LIMIT_BYTES = 62 * 1024 * 1024
LANES = 128
SUBLANES = 8
ROW_TILE = 1024
COL_TILE = 512
PROJ_COL_TILE = 1024
FF_TILE = 512
CONV_ROWS = 128
CONV_TILE_ROWS = 256
PROJ_SLAB = 256
CONV_HALO = 32
HEAD_SLAB = GROUP * HEAD_DIM

assert COL_TILE == KV_DIM and LANES == 2 * HEAD_DIM and HEAD_SLAB == 2 * LANES

BF16 = jnp.bfloat16
F32 = jnp.float32
NT_DIMS = (((1,), (1,)), ((), ()))


def _params(*sem):
    return pltpu.CompilerParams(dimension_semantics=sem, vmem_limit_bytes=VMEM_LIMIT_BYTES)


def _rms_rows(x, g):
    ms = jnp.mean(x * x, axis=-1, keepdims=True)
    return x * lax.rsqrt(ms + RMS_EPS) * g


def _layer_vec(v):
    return v.reshape(v.shape[0], 1, v.shape[1])


def _first_pass_tile(layer, n_tiles):
    def index_map(i, j):
        return (layer, 0, jnp.where(i == 0, j, n_tiles - 1))
    return index_map


def _last_tile_only(n_row_tiles, col_map):
    def index_map(i, j):
        return (0, jnp.where(i == n_row_tiles - 1, col_map(j), col_map(0)))
    return index_map


def _qkv_kernel(x_ref, xs_ref, g_ref, w_ref, b_ref, cos_ref, sin_ref, coss_ref, sins_ref,
                q_ref, kv_ref, qs_ref, kvs_ref, h_ref, hs_ref, wbf_ref):
    i = pl.program_id(0)
    j = pl.program_id(1)
    nq = Q_DIM // COL_TILE

    @pl.when(i == 0)
    def _():
        wbf_ref[j] = w_ref[...].astype(BF16)

    w = wbf_ref[j]
    bias = b_ref[...]
    reps = COL_TILE // LANES

    def normalize(x_r, h_r):
        h_r[...] = _rms_rows(x_r[...], g_ref[...]).astype(BF16)

    def project(h_r, rows):
        return jnp.dot(h_r[rows, :], w, preferred_element_type=F32) + bias

    def rotate_store(rows, y, cos_r, sin_r, q_r, kv_r):
        lane = lax.broadcasted_iota(jnp.int32, y.shape, 1)
        first_half = (lane % HEAD_DIM) < (HEAD_DIM // 2)
        rot = jnp.where(first_half,
                        pltpu.roll(y, COL_TILE - HEAD_DIM // 2, 1),
                        pltpu.roll(y, HEAD_DIM // 2, 1))
        r = (y * jnp.tile(cos_r[rows, :], (1, reps))
             + rot * jnp.tile(sin_r[rows, :], (1, reps)))

        @pl.when(j < nq)
        def _():
            q_r[rows, :] = r.astype(BF16)

        @pl.when(j >= nq)
        def _():
            kv_r[rows, :] = r

    @pl.when(j == 0)
    def _():
        normalize(x_ref, h_ref)

    half = h_ref.shape[0] // 2
    top, bottom = slice(0, half), slice(half, 2 * half)
    y_top = project(h_ref, top)
    y_bottom = project(h_ref, bottom)
    rotate_store(top, y_top, cos_ref, sin_ref, q_ref, kv_ref)
    rotate_store(bottom, y_bottom, cos_ref, sin_ref, q_ref, kv_ref)

    @pl.when(i == pl.num_programs(0) - 1)
    def _():
        @pl.when(j == 0)
        def _():
            normalize(xs_ref, hs_ref)

        rows = slice(0, hs_ref.shape[0])
        rotate_store(rows, project(hs_ref, rows), coss_ref, sins_ref, qs_ref, kvs_ref)


def _qkv_rope(x, xs, g, w, b, tabs, tabs_s, layer, norm_layer):
    m, ms = x.shape[0], xs.shape[0]
    tm = min(ROW_TILE, m)
    ni = m // tm
    cos_t, sin_t = tabs
    cos_s, sin_s = tabs_s
    n_tab = cos_t.shape[0] // tm
    nq = Q_DIM // COL_TILE
    nj = (Q_DIM + 2 * KV_DIM) // COL_TILE

    def tab_col(j):
        return jnp.maximum(j - nq + 1, 0)

    def q_col(j):
        return jnp.minimum(j, nq - 1)

    def kv_col(j):
        return jnp.maximum(j - nq, 0)

    def tab_map(i, j):
        return (i % n_tab, tab_col(j))

    return pl.pallas_call(
        _qkv_kernel,
        grid=(ni, nj),
        in_specs=[
            pl.BlockSpec((tm, D_MODEL), lambda i, j: (i, 0)),
            pl.BlockSpec((ms, D_MODEL), lambda i, j: (0, 0)),
            pl.BlockSpec((None, 1, D_MODEL), lambda i, j: (norm_layer, 0, 0)),
            pl.BlockSpec((None, D_MODEL, COL_TILE), _first_pass_tile(layer, nj)),
            pl.BlockSpec((None, 1, COL_TILE), lambda i, j: (layer, 0, j)),
            pl.BlockSpec((tm, LANES), tab_map),
            pl.BlockSpec((tm, LANES), tab_map),
            pl.BlockSpec((ms, LANES), _last_tile_only(ni, tab_col)),
            pl.BlockSpec((ms, LANES), _last_tile_only(ni, tab_col)),
        ],
        out_specs=[
            pl.BlockSpec((tm, COL_TILE), lambda i, j: (i, q_col(j))),
            pl.BlockSpec((tm, COL_TILE), lambda i, j: (i, kv_col(j))),
            pl.BlockSpec((ms, COL_TILE), _last_tile_only(ni, q_col)),
            pl.BlockSpec((ms, COL_TILE), _last_tile_only(ni, kv_col)),
        ],
        out_shape=[
            jax.ShapeDtypeStruct((m, Q_DIM), BF16),
            jax.ShapeDtypeStruct((m, 2 * KV_DIM), F32),
            jax.ShapeDtypeStruct((ms, Q_DIM), BF16),
            jax.ShapeDtypeStruct((ms, 2 * KV_DIM), F32),
        ],
        scratch_shapes=[pltpu.VMEM((tm, D_MODEL), BF16),
                        pltpu.VMEM((ms, D_MODEL), BF16),
                        pltpu.VMEM((nj, D_MODEL, COL_TILE), BF16)],
        compiler_params=_params("arbitrary", "arbitrary"),
        name="qkv_rope",
    )(x, xs, g, w, b, cos_t, sin_t, cos_s, sin_s)


def _rep_heads(x):
    n = x.shape[1]
    lane = lax.broadcasted_iota(jnp.int32, x.shape, 1)
    low = (lane % LANES) < HEAD_DIM
    even = jnp.where(low, x, pltpu.roll(x, HEAD_DIM, 1))
    odd = jnp.where(low, pltpu.roll(x, n - HEAD_DIM, 1), x)
    pieces = []
    for s in range(n // LANES):
        sl = slice(s * LANES, (s + 1) * LANES)
        pieces += [even[:, sl], even[:, sl], odd[:, sl], odd[:, sl]]
    return jnp.concatenate(pieces, axis=1)


ATTN_SEQS = 4


def _attn_prompt_kernel(sinks_ref, q_ref, kc_ref, vc_ref, o_ref, kb_ref, vt_ref, ot_ref, *, layer):
    n = pl.program_id(1)

    @pl.when(n == 0)
    def _():
        kb_ref[:, 0:WINDOW, :] = jnp.zeros((ATTN_SEQS, WINDOW, Q_DIM), BF16)
        vt_ref[:, :, 0:WINDOW] = jnp.zeros((ATTN_SEQS, KV_DIM, WINDOW), BF16)

    @pl.when(n > 0)
    def _():
        kb_ref[:, 0:WINDOW, :] = kb_ref[:, WINDOW:, :]
        vt_ref[:, :, 0:WINDOW] = vt_ref[:, :, WINDOW:]

    for b in range(ATTN_SEQS):
        kb_ref[b, WINDOW:, :] = _rep_heads(kc_ref[b]).astype(BF16)
        vt_ref[b, :, WINDOW:] = vc_ref[b].T.astype(BF16)

    cols = GROUP * WINDOW
    key = lax.broadcasted_iota(jnp.int32, (2 * WINDOW, cols), 0)
    qi = lax.broadcasted_iota(jnp.int32, (2 * WINDOW, cols), 1) % WINDOW
    allowed = jnp.logical_or(
        jnp.logical_and(key < WINDOW, jnp.logical_and(key >= qi, n > 0)),
        jnp.logical_and(key >= WINDOW, key - WINDOW <= qi))
    bias = jnp.where(allowed, 0.0, -jnp.inf).astype(F32)
    lane_group = lax.broadcasted_iota(jnp.int32, (WINDOW, HEAD_SLAB), 1) // HEAD_DIM
    col_group = lax.broadcasted_iota(jnp.int32, (1, cols), 1) // WINDOW

    def scores(b, h):
        hs = slice(h * HEAD_SLAB, (h + 1) * HEAD_SLAB)
        qh = q_ref[b, :, hs]
        zero = jnp.zeros_like(qh)
        qm = jnp.concatenate([jnp.where(lane_group == g, qh, zero) for g in range(GROUP)], axis=0)
        return lax.dot_general(kb_ref[b, :, hs], qm, NT_DIMS, preferred_element_type=F32) + bias

    def sink_row(h):
        sink = jnp.full((1, cols), sinks_ref[layer, h * GROUP + GROUP - 1], F32)
        for g in range(GROUP - 2, -1, -1):
            sink = jnp.where(col_group == g, sinks_ref[layer, h * GROUP + g], sink)
        return sink

    def softmax(s, sink):
        mx = jnp.maximum(jnp.max(s, axis=0, keepdims=True), sink)
        p = jnp.exp(s - mx)
        denom = jnp.sum(p, axis=0, keepdims=True) + jnp.exp(sink - mx)
        return p.astype(BF16), 1.0 / denom

    def finish(b, h, p, inv):
        ot = jnp.dot(vt_ref[b, h * HEAD_DIM:(h + 1) * HEAD_DIM, :], p,
                     preferred_element_type=F32) * inv
        for g in range(GROUP):
            r0 = h * HEAD_SLAB + g * HEAD_DIM
            ot_ref[b, r0:r0 + HEAD_DIM, :] = ot[:, g * WINDOW:(g + 1) * WINDOW]

    work = [(b, h) for b in range(ATTN_SEQS) for h in range(N_KV_HEADS)]
    s_next = scores(*work[0])
    pending = None
    for idx, (b, h) in enumerate(work):
        s = s_next
        if idx + 1 < len(work):
            s_next = scores(*work[idx + 1])
        p, inv = softmax(s, sink_row(h))
        if pending is not None:
            finish(*pending)
        pending = (b, h, p, inv)
    finish(*pending)
    for b in range(ATTN_SEQS):
        o_ref[b] = ot_ref[b].T.astype(BF16)


def _attn_prompt(q, kv, sinks, batch, seq, layer):
    nb = seq // WINDOW
    q3 = q.reshape(batch, seq, Q_DIM)
    kv3 = kv.reshape(batch, seq, 2 * KV_DIM)
    out = pl.pallas_call(
        functools.partial(_attn_prompt_kernel, layer=layer),
        grid=(batch // ATTN_SEQS, nb),
        in_specs=[
            pl.BlockSpec(memory_space=pltpu.SMEM),
            pl.BlockSpec((ATTN_SEQS, WINDOW, Q_DIM), lambda b, n: (b, n, 0)),
            pl.BlockSpec((ATTN_SEQS, WINDOW, KV_DIM), lambda b, n: (b, n, 0)),
            pl.BlockSpec((ATTN_SEQS, WINDOW, KV_DIM), lambda b, n: (b, n, 1)),
        ],
        out_specs=pl.BlockSpec((ATTN_SEQS, WINDOW, Q_DIM), lambda b, n: (b, n, 0)),
        out_shape=jax.ShapeDtypeStruct((batch, seq, Q_DIM), BF16),
        scratch_shapes=[pltpu.VMEM((ATTN_SEQS, 2 * WINDOW, Q_DIM), BF16),
                        pltpu.VMEM((ATTN_SEQS, KV_DIM, 2 * WINDOW), BF16),
                        pltpu.VMEM((ATTN_SEQS, Q_DIM, WINDOW), F32)],
        compiler_params=_params("parallel", "arbitrary"),
        name="attn_prompt",
    )(sinks, q3, kv3, kv3)
    return out.reshape(batch * seq, Q_DIM)


SAMPLE_BATCH_BLOCK = 8


def _attn_sample_kernel(sink_ref, q_ref, kvn_ref, ck_ref, cv_ref, o_ref, own_ref, *, t_new):
    pad = 2 * SUBLANES - t_new
    n_keys = WINDOW + t_new + pad
    rows = N_HEADS * t_new
    t_of_row = lax.broadcasted_iota(jnp.int32, (rows, n_keys), 0) % t_new
    col = lax.broadcasted_iota(jnp.int32, (rows, n_keys), 1)
    allowed = jnp.logical_or(
        jnp.logical_and(col < WINDOW, col >= t_of_row),
        jnp.logical_and(col >= WINDOW, col - WINDOW <= t_of_row))
    bias = jnp.where(allowed, 0.0, -jnp.inf).astype(F32)
    head_of_row = lax.broadcasted_iota(jnp.int32, (rows, Q_DIM), 0) // t_new
    head_of_lane = lax.broadcasted_iota(jnp.int32, (rows, Q_DIM), 1) // HEAD_DIM
    own_ref[...] = jnp.where(head_of_row == head_of_lane, 1.0, 0.0).astype(F32)
    sink = sink_ref[...]
    q_all = q_ref[...].astype(F32)
    kvn = kvn_ref[...]
    zpad = jnp.zeros((pad, KV_DIM), F32)

    def scores(bb):
        r0 = bb * t_new
        k_new = jnp.concatenate([kvn[r0:r0 + t_new, :KV_DIM], zpad], axis=0)
        k_all = jnp.concatenate([_rep_heads(ck_ref[bb]).astype(BF16),
                                 _rep_heads(k_new).astype(BF16)], axis=0)
        qf = (jnp.tile(q_all[r0:r0 + t_new], (N_HEADS, 1)) * own_ref[...]).astype(BF16)
        return lax.dot_general(qf, k_all, NT_DIMS, preferred_element_type=F32) + bias

    def softmax(s):
        mx = jnp.maximum(jnp.max(s, axis=-1, keepdims=True), sink)
        p = jnp.exp(s - mx)
        denom = jnp.sum(p, axis=-1, keepdims=True) + jnp.exp(sink - mx)
        return (p * (1.0 / denom)).astype(BF16)

    def finish(bb, pn):
        r0 = bb * t_new
        v_new = jnp.concatenate([kvn[r0:r0 + t_new, KV_DIM:], zpad], axis=0)
        v_all = jnp.concatenate([_rep_heads(cv_ref[bb]).astype(BF16),
                                 _rep_heads(v_new).astype(BF16)], axis=0)
        pv = jnp.dot(pn, v_all, preferred_element_type=F32) * own_ref[...]
        o = pv[0:t_new]
        for h in range(1, N_HEADS):
            o = o + pv[h * t_new:(h + 1) * t_new]
        o_ref[r0:r0 + t_new, :] = o.astype(BF16)

    s_next = scores(0)
    pending = None
    for bb in range(SAMPLE_BATCH_BLOCK):
        s = s_next
        if bb + 1 < SAMPLE_BATCH_BLOCK:
            s_next = scores(bb + 1)
        pn = softmax(s)
        if pending is not None:
            finish(*pending)
        pending = (bb, pn)
    finish(*pending)


def _attn_sample(q, kv, cache_k, cache_v, sink_rows, batch, t_new, layer):
    rows = SAMPLE_BATCH_BLOCK * t_new
    cache_spec = pl.BlockSpec((None, SAMPLE_BATCH_BLOCK, WINDOW, KV_DIM), lambda b: (layer, b, 0, 0))
    return pl.pallas_call(
        functools.partial(_attn_sample_kernel, t_new=t_new),
        grid=(batch // SAMPLE_BATCH_BLOCK,),
        in_specs=[
            pl.BlockSpec((None, N_HEADS * t_new, 1), lambda b: (layer, 0, 0)),
            pl.BlockSpec((rows, Q_DIM), lambda b: (b, 0)),
            pl.BlockSpec((rows, 2 * KV_DIM), lambda b: (b, 0)),
            cache_spec,
            cache_spec,
        ],
        out_specs=pl.BlockSpec((rows, Q_DIM), lambda b: (b, 0)),
        out_shape=jax.ShapeDtypeStruct((batch * t_new, Q_DIM), BF16),
        scratch_shapes=[pltpu.VMEM((N_HEADS * t_new, Q_DIM), F32)],
        compiler_params=_params("parallel"),
        name="attn_sample",
    )(sink_rows, q, kv, cache_k, cache_v)


def _proj_res_kernel(a_ref, as_ref, w_ref, b_ref, res_ref, ress_ref, o_ref, os_ref, wbf_ref):
    i = pl.program_id(0)
    j = pl.program_id(1)

    @pl.when(i == 0)
    def _():
        wbf_ref[j] = w_ref[...].astype(BF16)

    def project(a_r, res_r, o_r):
        y = jnp.dot(a_r[...], wbf_ref[j], preferred_element_type=F32)
        o_r[...] = res_r[...] + y + b_ref[...]

    project(a_ref, res_ref, o_ref)

    @pl.when(i == pl.num_programs(0) - 1)
    def _():
        project(as_ref, ress_ref, os_ref)


def _proj_res(a, a_s, w, b, res, res_s, layer):
    m, k = a.shape
    ms = a_s.shape[0]
    n = w.shape[-1]
    tm = min(ROW_TILE, m)
    ni = m // tm
    nj = n // PROJ_COL_TILE
    sample_tile = _last_tile_only(ni, lambda j: j)
    return pl.pallas_call(
        _proj_res_kernel,
        grid=(ni, nj),
        in_specs=[
            pl.BlockSpec((tm, k), lambda i, j: (i, 0)),
            pl.BlockSpec((ms, k), lambda i, j: (0, 0)),
            pl.BlockSpec((None, k, PROJ_COL_TILE), _first_pass_tile(layer, nj)),
            pl.BlockSpec((None, 1, PROJ_COL_TILE), lambda i, j: (layer, 0, j)),
            pl.BlockSpec((tm, PROJ_COL_TILE), lambda i, j: (i, j)),
            pl.BlockSpec((ms, PROJ_COL_TILE), sample_tile),
        ],
        out_specs=[
            pl.BlockSpec((tm, PROJ_COL_TILE), lambda i, j: (i, j)),
            pl.BlockSpec((ms, PROJ_COL_TILE), sample_tile),
        ],
        out_shape=[
            jax.ShapeDtypeStruct((m, n), F32),
            jax.ShapeDtypeStruct((ms, n), F32),
        ],
        scratch_shapes=[pltpu.VMEM((nj, k, PROJ_COL_TILE), BF16)],
        compiler_params=_params("arbitrary", "arbitrary"),
        name="proj_res",
    )(a, a_s, w, b, res, res_s)


def _mlp_kernel(x_ref, xs_ref, g_ref, wu_ref, wd_ref, gf_ref, o_ref, os_ref, h_ref, hs_ref, *,
                final_norm):
    i = pl.program_id(0)
    f = pl.program_id(1)
    last_i = pl.num_programs(0) - 1
    last_f = pl.num_programs(1) - 1

    def start(x_r, h_r, o_r):
        x = x_r[...]
        h_r[...] = _rms_rows(x, g_ref[...]).astype(BF16)
        o_r[...] = x

    def accumulate(h_r, o_r):
        a = jnp.dot(h_r[...], wu_ref[...].astype(BF16), preferred_element_type=F32)
        a = jnp.maximum(a, 0.0)
        a = (a * a).astype(BF16)
        o_r[...] += jnp.dot(a, wd_ref[...].astype(BF16), preferred_element_type=F32)

    def finish(o_r):
        o_r[...] = _rms_rows(o_r[...], gf_ref[...])

    @pl.when(f == 0)
    def _():
        start(x_ref, h_ref, o_ref)

    accumulate(h_ref, o_ref)

    if final_norm:
        @pl.when(f == last_f)
        def _():
            finish(o_ref)

    @pl.when(i == last_i)
    def _():
        @pl.when(f == 0)
        def _():
            start(xs_ref, hs_ref, os_ref)

        accumulate(hs_ref, os_ref)

        if final_norm:
            @pl.when(f == last_f)
            def _():
                finish(os_ref)


def _mlp(x, xs, g, w_up, w_down, g_final, layer, final_norm):
    m = x.shape[0]
    ms = xs.shape[0]
    tm = min(ROW_TILE, m)
    return pl.pallas_call(
        functools.partial(_mlp_kernel, final_norm=final_norm),
        grid=(m // tm, D_FF // FF_TILE),
        in_specs=[
            pl.BlockSpec((tm, D_MODEL), lambda i, f: (i, 0)),
            pl.BlockSpec((ms, D_MODEL), lambda i, f: (0, 0)),
            pl.BlockSpec((None, 1, D_MODEL), lambda i, f: (layer, 0, 0)),
            pl.BlockSpec((None, D_MODEL, FF_TILE), lambda i, f: (layer, 0, f)),
            pl.BlockSpec((None, FF_TILE, D_MODEL), lambda i, f: (layer, f, 0)),
            pl.BlockSpec((1, D_MODEL), lambda i, f: (0, 0)),
        ],
        out_specs=[
            pl.BlockSpec((tm, D_MODEL), lambda i, f: (i, 0)),
            pl.BlockSpec((ms, D_MODEL), lambda i, f: (0, 0)),
        ],
        out_shape=[
            jax.ShapeDtypeStruct((m, D_MODEL), F32),
            jax.ShapeDtypeStruct((ms, D_MODEL), F32),
        ],
        scratch_shapes=[pltpu.VMEM((tm, D_MODEL), BF16), pltpu.VMEM((ms, D_MODEL), BF16)],
        compiler_params=pltpu.CompilerParams(dimension_semantics=("arbitrary", "arbitrary"),
                                             vmem_limit_bytes=MLP_VMEM_LIMIT_BYTES),
        name="mlp",
    )(x, xs, g, w_up, w_down, g_final)


def _glu_kernel(x_ref, g_ref, wa_ref, wg_ref, ba_ref, bg_ref, u_ref, h_ref, wbf_ref):
    j = pl.program_id(1)
    nj = D_MODEL // COL_TILE

    @pl.when(pl.program_id(0) == 0)
    def _():
        wbf_ref[j] = wa_ref[...].astype(BF16)
        wbf_ref[nj + j] = wg_ref[...].astype(BF16)

    @pl.when(j == 0)
    def _():
        h_ref[...] = _rms_rows(x_ref[...], g_ref[...]).astype(BF16)

    wa, wg = wbf_ref[j], wbf_ref[nj + j]

    def project(rows):
        h = h_ref[rows, :]
        return (jnp.dot(h, wa, preferred_element_type=F32) + ba_ref[...],
                jnp.dot(h, wg, preferred_element_type=F32) + bg_ref[...])

    half = h_ref.shape[0] // 2
    top, bottom = slice(0, half), slice(half, 2 * half)
    a_top, gate_top = project(top)
    a_bottom, gate_bottom = project(bottom)
    u_ref[top, :] = a_top * jax.nn.sigmoid(gate_top)
    u_ref[bottom, :] = a_bottom * jax.nn.sigmoid(gate_bottom)


def _glu(x, g, w_in, b_in, layer, norm_layer):
    m = x.shape[0]
    tm = min(ROW_TILE, m)
    nj = D_MODEL // COL_TILE

    def gate_tile(i, j):
        return (layer, 0, nj + jnp.where(i == 0, j, nj - 1))

    return pl.pallas_call(
        _glu_kernel,
        grid=(m // tm, nj),
        in_specs=[
            pl.BlockSpec((tm, D_MODEL), lambda i, j: (i, 0)),
            pl.BlockSpec((None, 1, D_MODEL), lambda i, j: (norm_layer, 0, 0)),
            pl.BlockSpec((None, D_MODEL, COL_TILE), _first_pass_tile(layer, nj)),
            pl.BlockSpec((None, D_MODEL, COL_TILE), gate_tile),
            pl.BlockSpec((None, 1, COL_TILE), lambda i, j: (layer, 0, j)),
            pl.BlockSpec((None, 1, COL_TILE), lambda i, j: (layer, 0, j + nj)),
        ],
        out_specs=pl.BlockSpec((tm, COL_TILE), lambda i, j: (i, j)),
        out_shape=jax.ShapeDtypeStruct((m, D_MODEL), F32),
        scratch_shapes=[pltpu.VMEM((tm, D_MODEL), BF16),
                        pltpu.VMEM((2 * nj, D_MODEL, COL_TILE), BF16)],
        compiler_params=_params("arbitrary", "arbitrary"),
        name="glu",
    )(x, g, w_in, w_in, b_in, b_in)


def _ln_swish(c, g, b):
    mu = jnp.mean(c, axis=-1, keepdims=True)
    xc = c - mu
    var = jnp.mean(xc * xc, axis=-1, keepdims=True)
    y = xc * lax.rsqrt(var + LN_EPS) * g + b
    return y * jax.nn.sigmoid(y)


def _conv_lanes(win_ref, row0, w_ref, vec_ref, c_ref, c):
    off = CONV_HALO - (CONV_WIDTH - 1)
    n_win = CONV_HALO + CONV_ROWS
    cs = slice(c * LANES, (c + 1) * LANES)
    win = win_ref[row0:row0 + n_win, cs]
    acc = jnp.zeros((CONV_ROWS, LANES), F32)
    for phase in range(SUBLANES):
        taps = [j for j in range(CONV_WIDTH) if (off + j) % SUBLANES == phase]
        if not taps:
            continue
        shifted = win if phase == 0 else pltpu.roll(win, n_win - phase, 0)
        for j in taps:
            base = (off + j) - phase
            acc = acc + shifted[base:base + CONV_ROWS] * w_ref[j:j + 1, cs]
    c_ref[:, cs] = acc + vec_ref[0:1, cs]


def _conv_proj_kernel(cur_ref, halo_ref, wdw_ref, vec_ref, w_ref, res_ref,
                      as_ref, ress_ref, o_ref, os_ref, win_ref, conv_ref, c_ref, wbf_ref, *,
                      tiles_per_seq):
    s = pl.program_id(0)
    n_tiles = pl.num_programs(0) - 1

    @pl.when(s == 0)
    def _():
        for c in range(D_MODEL // COL_TILE):
            cs = slice(c * COL_TILE, (c + 1) * COL_TILE)
            wbf_ref[:, cs] = w_ref[:, cs].astype(BF16)
        c_ref[...] = jnp.zeros(c_ref.shape, BF16)

    tile = jnp.minimum(s, n_tiles - 1)
    halo = halo_ref[...]
    win_ref[0:CONV_HALO, :] = jnp.where(tile % tiles_per_seq > 0, halo, jnp.zeros_like(halo))
    win_ref[CONV_HALO:, :] = cur_ref[...]

    def project(n):
        ns = slice(n * PROJ_SLAB, (n + 1) * PROJ_SLAB)
        y = jnp.dot(c_ref[(s + 1) % 2], wbf_ref[:, ns], preferred_element_type=F32)
        o_ref[:, ns] = res_ref[:, ns] + y + vec_ref[3:4, ns]

    lane_groups = D_MODEL // LANES
    row_blocks = CONV_TILE_ROWS // CONV_ROWS
    n_slabs = D_MODEL // PROJ_SLAB
    every = (row_blocks * lane_groups) // n_slabs
    slot = s % 2
    for k in range(row_blocks):
        for c in range(lane_groups):
            idx = k * lane_groups + c
            if idx % every == 0:
                project(idx // every)
            _conv_lanes(win_ref, k * CONV_ROWS, wdw_ref, vec_ref, conv_ref, c)
        c_ref[slot, k * CONV_ROWS:(k + 1) * CONV_ROWS, :] = _ln_swish(
            conv_ref[...], vec_ref[1:2, :], vec_ref[2:3, :]).astype(BF16)

    @pl.when(s == n_tiles)
    def _():
        ys = jnp.dot(as_ref[...], wbf_ref[...], preferred_element_type=F32)
        os_ref[...] = ress_ref[...] + ys + vec_ref[3:4, :]


def _conv_proj(u, a_s, res, res_s, w_dw, conv_vecs, w_out, seq, layer):
    m = u.shape[0]
    ms = a_s.shape[0]
    n_tiles = m // CONV_TILE_ROWS
    ratio = CONV_TILE_ROWS // CONV_HALO
    const = lambda shape: pl.BlockSpec(shape, lambda s: (0, 0))

    def cur_tile(s):
        return jnp.minimum(s, n_tiles - 1)

    def prev_tile(s):
        return jnp.maximum(s - 1, 0)

    return pl.pallas_call(
        functools.partial(_conv_proj_kernel, tiles_per_seq=seq // CONV_TILE_ROWS),
        grid=(n_tiles + 1,),
        in_specs=[
            pl.BlockSpec((CONV_TILE_ROWS, D_MODEL), lambda s: (cur_tile(s), 0)),
            pl.BlockSpec((CONV_HALO, D_MODEL), lambda s: (jnp.maximum(cur_tile(s) * ratio - 1, 0), 0)),
            pl.BlockSpec((None, CONV_WIDTH, D_MODEL), lambda s: (layer, 0, 0)),
            pl.BlockSpec((None, 4, D_MODEL), lambda s: (layer, 0, 0)),
            pl.BlockSpec((None, D_MODEL, D_MODEL), lambda s: (layer, 0, 0)),
            pl.BlockSpec((CONV_TILE_ROWS, D_MODEL), lambda s: (prev_tile(s), 0)),
            const((ms, D_MODEL)),
            const((ms, D_MODEL)),
        ],
        out_specs=[
            pl.BlockSpec((CONV_TILE_ROWS, D_MODEL), lambda s: (prev_tile(s), 0)),
            const((ms, D_MODEL)),
        ],
        out_shape=[
            jax.ShapeDtypeStruct((m, D_MODEL), F32),
            jax.ShapeDtypeStruct((ms, D_MODEL), F32),
        ],
        scratch_shapes=[pltpu.VMEM((CONV_HALO + CONV_TILE_ROWS, D_MODEL), F32),
                        pltpu.VMEM((CONV_ROWS, D_MODEL), F32),
                        pltpu.VMEM((2, CONV_TILE_ROWS, D_MODEL), BF16),
                        pltpu.VMEM((D_MODEL, D_MODEL), BF16)],
        compiler_params=_params("arbitrary"),
        name="conv_proj",
    )(u, u, w_dw, conv_vecs, w_out, res, a_s, res_s)


CONV_SAMPLE_BATCH_BLOCK = 4
CONV_NEW_ROW0 = 32


def _conv_sample_kernel(state_ref, u_ref, w_ref, vec_ref, o_ref, win_ref, *, t_new):
    keep = CONV_WIDTH - 1
    gap = CONV_NEW_ROW0 - keep
    row = lax.broadcasted_iota(jnp.int32, (t_new, D_MODEL), 0)
    outs = []
    for bb in range(CONV_SAMPLE_BATCH_BLOCK):
        win_ref[bb, CONV_NEW_ROW0 - SUBLANES:CONV_NEW_ROW0, :] = jnp.zeros((SUBLANES, D_MODEL), F32)
        win_ref[bb, 0:keep, :] = state_ref[bb]
        win_ref[bb, CONV_NEW_ROW0:CONV_NEW_ROW0 + t_new, :] = u_ref[bb * t_new:(bb + 1) * t_new, :]
        acc = jnp.zeros((t_new, D_MODEL), F32)
        for j in range(CONV_WIDTH):
            if j + t_new <= keep:
                rows = win_ref[bb, j:j + t_new, :]
            else:
                old = win_ref[bb, j:j + t_new, :]
                fresh = win_ref[bb, j + gap:j + gap + t_new, :]
                rows = jnp.where(row + j < keep, old, fresh)
            acc = acc + rows * w_ref[j:j + 1, :]
        outs.append(acc + vec_ref[0:1, :])
    c = jnp.concatenate(outs, axis=0)
    o_ref[...] = _ln_swish(c, vec_ref[1:2, :], vec_ref[2:3, :]).astype(BF16)


def _conv_sample(state, u_s, w_dw, conv_vecs, layer):
    batch = state.shape[1]
    t_new = u_s.shape[0] // batch
    keep = CONV_WIDTH - 1
    return pl.pallas_call(
        functools.partial(_conv_sample_kernel, t_new=t_new),
        grid=(batch // CONV_SAMPLE_BATCH_BLOCK,),
        in_specs=[
            pl.BlockSpec((None, CONV_SAMPLE_BATCH_BLOCK, keep, D_MODEL), lambda b: (layer, b, 0, 0)),
            pl.BlockSpec((CONV_SAMPLE_BATCH_BLOCK * t_new, D_MODEL), lambda b: (b, 0)),
            pl.BlockSpec((None, CONV_WIDTH, D_MODEL), lambda b: (layer, 0, 0)),
            pl.BlockSpec((None, 4, D_MODEL), lambda b: (layer, 0, 0)),
        ],
        out_specs=pl.BlockSpec((CONV_SAMPLE_BATCH_BLOCK * t_new, D_MODEL), lambda b: (b, 0)),
        out_shape=jax.ShapeDtypeStruct((batch * t_new, D_MODEL), BF16),
        scratch_shapes=[pltpu.VMEM((CONV_SAMPLE_BATCH_BLOCK, CONV_NEW_ROW0 + t_new, D_MODEL), F32)],
        compiler_params=_params("parallel"),
        name="conv_sample",
    )(state, u_s, w_dw, conv_vecs)


def _rope_tables(pos):
    inv = ROPE_THETA ** (-jnp.arange(0, HEAD_DIM, 2, dtype=F32) / HEAD_DIM)
    ang = pos.astype(F32)[:, None] * inv[None, :]
    cos, sin = jnp.cos(ang), jnp.sin(ang)
    reps = LANES // HEAD_DIM
    cos_t = jnp.tile(jnp.concatenate([cos, cos], axis=-1), (1, reps))
    sin_t = jnp.tile(jnp.concatenate([-sin, sin], axis=-1), (1, reps))
    scale = HEAD_DIM ** -0.5
    cos_all = jnp.concatenate([cos_t * scale, cos_t, jnp.ones_like(cos_t)], axis=1)
    sin_all = jnp.concatenate([sin_t * scale, sin_t, jnp.zeros_like(sin_t)], axis=1)
    return cos_all, sin_all


def kernel(x_prompt, x_sample, cache_k, cache_v, state_conv, norm_mix, norm_mlp, norm_final,
           attn_w_qkv, attn_b_qkv, attn_sinks, attn_w_o, attn_b_o,
           conv_w_in, conv_b_in, conv_w_dw, conv_b_dw, conv_ln_g, conv_ln_b,
           conv_w_out, conv_b_out, mlp_w_up, mlp_w_down):
    batch, seq, _ = x_prompt.shape
    dbatch, t_new, _ = x_sample.shape
    xp = x_prompt.reshape(batch * seq, D_MODEL)
    xs = x_sample.reshape(dbatch * t_new, D_MODEL)

    cos_p, sin_p = _rope_tables(jnp.arange(seq, dtype=jnp.int32))
    cos_s, sin_s = _rope_tables(PAST_LEN + jnp.arange(t_new, dtype=jnp.int32))
    cos_s = jnp.tile(cos_s, (dbatch, 1))
    sin_s = jnp.tile(sin_s, (dbatch, 1))

    norm_mix, norm_mlp = _layer_vec(norm_mix), _layer_vec(norm_mlp)
    attn_b_qkv, attn_b_o = _layer_vec(attn_b_qkv), _layer_vec(attn_b_o)
    conv_b_in = _layer_vec(conv_b_in)
    conv_vecs = jnp.stack([conv_b_dw, conv_ln_g, conv_ln_b, conv_b_out], axis=1)
    n_attn = cache_k.shape[0]
    ck = cache_k.reshape(n_attn, dbatch, WINDOW, KV_DIM)
    cv = cache_v.reshape(n_attn, dbatch, WINDOW, KV_DIM)
    sink_rows = jnp.repeat(attn_sinks, t_new, axis=1)[:, :, None]
    g_final = norm_final.reshape(1, D_MODEL)

    k_p, v_p, c_p, k_new, v_new, u_new = [], [], [], [], [], []
    for i in range(DEPTH):
        j = i // 2
        if i % 2 == 0:
            q, kv, q_s, kv_s = _qkv_rope(xp, xs, norm_mix, attn_w_qkv, attn_b_qkv,
                                         (cos_p, sin_p), (cos_s, sin_s), j, i)
            o = _attn_prompt(q, kv, attn_sinks, batch, seq, j)
            o_s = _attn_sample(q_s, kv_s, ck, cv, sink_rows, dbatch, t_new, j)
            xp, xs = _proj_res(o, o_s, attn_w_o, attn_b_o, xp, xs, j)
            kv3 = kv.reshape(batch, seq, 2 * KV_DIM)[:, seq - WINDOW:]
            k_p.append(kv3[..., :KV_DIM].reshape(batch, WINDOW, N_KV_HEADS, HEAD_DIM))
            v_p.append(kv3[..., KV_DIM:].reshape(batch, WINDOW, N_KV_HEADS, HEAD_DIM))
            kv3 = kv_s.reshape(dbatch, t_new, 2 * KV_DIM)
            k_new.append(kv3[..., :KV_DIM].reshape(dbatch, t_new, N_KV_HEADS, HEAD_DIM))
            v_new.append(kv3[..., KV_DIM:].reshape(dbatch, t_new, N_KV_HEADS, HEAD_DIM))
        else:
            u = _glu(xp, norm_mix, conv_w_in, conv_b_in, j, i)
            u_s = _glu(xs, norm_mix, conv_w_in, conv_b_in, j, i)
            c_smp = _conv_sample(state_conv, u_s, conv_w_dw, conv_vecs, j)
            xp, xs = _conv_proj(u, c_smp, xp, xs, conv_w_dw, conv_vecs, conv_w_out, seq, j)
            c_p.append(u.reshape(batch, seq, D_MODEL)[:, seq - (CONV_WIDTH - 1):])
            u_new.append(u_s.reshape(dbatch, t_new, D_MODEL))
        xp, xs = _mlp(xp, xs, norm_mlp, mlp_w_up, mlp_w_down, g_final, i, i == DEPTH - 1)

    k_s = jnp.concatenate([cache_k[:, :, t_new:], jnp.stack(k_new)], axis=2)
    v_s = jnp.concatenate([cache_v[:, :, t_new:], jnp.stack(v_new)], axis=2)
    c_s = jnp.concatenate([state_conv[:, :, t_new:], jnp.stack(u_new)], axis=2)

    y_p = xp.reshape(batch, seq, D_MODEL)
    y_s = xs.reshape(dbatch, t_new, D_MODEL)
    return (y_p, y_s, jnp.stack(k_p), jnp.stack(v_p), jnp.stack(c_p), k_s, v_s, c_s)
```
